```python
import jax
import jax.numpy as jnp
from jax import lax
import numpy as np

D_MODEL = 1024
BATCH = 16
SEQ = 4096
DEPTH = 4

GRID_W = 64
CTX_LEN = 256
HEAD_DIM = 64
ROPE_THETA = 10000.0
LN_EPS = 1e-6
MASK_VALUE = -1e30

CONV_CH = 512
CONV_K = 31
SWA_HEADS = 8
SWA_KV_HEADS = 2
SWA_GROUP = SWA_HEADS // SWA_KV_HEADS
SWA_WINDOW = 128
SWA_BLOCK = 128
POOL_CH = 512
POOL_WINDOWS = (2, 4, 8, 16)
POOL_GROUP = POOL_CH // len(POOL_WINDOWS)
MLA_HEADS = 8
MLA_Q_RANK = 256
MLA_KV_RANK = 128
MLA_NOPE = 64
MLA_ROPE = 32
MLA_V = 64
MLA_BLOCK = 128
N_BRANCH = 4
BRANCH_W = 512
N_EXPERTS = 32
TOP_K = 4
D_EXPERT = 1024
SWIGLU_LIMIT = 7.0
SWIGLU_ALPHA = 1.702
EXPERT_BLOCK = 256
DN_ALPHA = (2 * DEPTH) ** 0.25
DN_BETA = (8 * DEPTH) ** -0.25

IN_SIZES = (2 * CONV_CH, SWA_HEADS * HEAD_DIM, SWA_KV_HEADS * HEAD_DIM, SWA_KV_HEADS * HEAD_DIM,
            POOL_CH, MLA_Q_RANK, MLA_KV_RANK, MLA_ROPE)
IN_SPLITS = tuple(int(s) for s in np.cumsum(IN_SIZES)[:-1])
IN_W = int(sum(IN_SIZES))

kernel_name = 'hybrid_gated_mixers_moe_dit_block'


def layer_norm(x, g=None, b=None):
    xf = x.astype(jnp.float32)
    mu = xf.mean(-1, keepdims=True)
    var = jnp.square(xf - mu).mean(-1, keepdims=True)
    y = (xf - mu) * lax.rsqrt(var + LN_EPS)
    if g is not None:
        y = y * g.astype(jnp.float32) + b.astype(jnp.float32)
    return y.astype(x.dtype)


def rms_norm(x, g):
    xf = x.astype(jnp.float32)
    y = xf * lax.rsqrt(jnp.mean(xf * xf, -1, keepdims=True) + LN_EPS)
    return (y * g.astype(jnp.float32)).astype(x.dtype)


def axial_rope(seq_len, rot_dim):
    rows = seq_len // GRID_W
    row = jnp.repeat(jnp.arange(rows, dtype=jnp.float32), GRID_W)
    col = jnp.tile(jnp.arange(GRID_W, dtype=jnp.float32), rows)
    n_freq = rot_dim // 4
    inv = ROPE_THETA ** (-jnp.arange(n_freq, dtype=jnp.float32) / n_freq)
    ang = jnp.concatenate([row[:, None] * inv, col[:, None] * inv], axis=-1)
    return jnp.cos(ang), jnp.sin(ang)


def apply_rope(x, cos, sin):
    half = x.shape[-1] // 2
    xf = x.astype(jnp.float32)
    x1, x2 = xf[..., :half], xf[..., half:]
    cs, sn = cos[:, None, :], sin[:, None, :]
    return jnp.concatenate([x1 * cs - x2 * sn, x1 * sn + x2 * cs], axis=-1).astype(x.dtype)


def ada_terms(cvec, w, b):
    m = jax.nn.silu(cvec) @ w + b
    m = m.reshape(cvec.shape[:-1] + (1, 6, D_MODEL))
    return tuple(m[..., i, :] for i in range(6))


def modulate(x, shift, scale):
    return layer_norm(x) * (1 + scale) + shift


def conv_module(a, dw_w, dw_b, ln_g, ln_b):
    val, gate = jnp.split(a, 2, axis=-1)
    v = val * jax.nn.sigmoid(gate)
    y = lax.conv_general_dilated(v, dw_w[:, None, :], (1,), [(CONV_K // 2, CONV_K // 2)],
                                 dimension_numbers=('NWC', 'WIO', 'NWC'),
                                 feature_group_count=CONV_CH) + dw_b
    return jax.nn.silu(layer_norm(y, ln_g, ln_b))


def pool_mixer(u, pool_w, pool_scale):
    bsz, L, _ = u.shape
    cs = jnp.concatenate([jnp.zeros((bsz, 1, POOL_CH), jnp.float32),
                          jnp.cumsum(u.astype(jnp.float32), axis=1)], axis=1)
    t = jnp.arange(L)
    means = []
    for g, w in enumerate(POOL_WINDOWS):
        lo = jnp.clip(t - w // 2, 0, L)
        hi = jnp.clip(t + w // 2, 0, L)
        seg = cs[:, :, g * POOL_GROUP:(g + 1) * POOL_GROUP]
        means.append((seg[:, hi] - seg[:, lo]) / (hi - lo).astype(jnp.float32)[:, None])
    pooled = jnp.concatenate(means, axis=-1).astype(u.dtype) - u
    y = jnp.einsum('blgc,gcd->blgd', pooled.reshape(bsz, L, len(POOL_WINDOWS), POOL_GROUP), pool_w)
    return y.reshape(bsz, L, POOL_CH) * pool_scale


def sink_softmax(scores, sink):
    sk = jnp.broadcast_to(sink.astype(jnp.float32).reshape(SWA_KV_HEADS, SWA_GROUP)[None, :, :, None, None],
                          scores.shape[:-1] + (1,))
    return jax.nn.softmax(jnp.concatenate([scores, sk], axis=-1), axis=-1)[..., :-1]


def swa_latent(q, k, v, kc, vc, sink):
    bsz, L = q.shape[:2]
    nb = L // SWA_BLOCK
    span = SWA_BLOCK + 2 * SWA_WINDOW
    pad = ((0, 0), (SWA_WINDOW, SWA_WINDOW), (0, 0), (0, 0))
    kp, vp = jnp.pad(k, pad), jnp.pad(v, pad)
    r = jnp.arange(SWA_BLOCK)[:, None]
    s = jnp.arange(span)[None, :]
    band = (s >= r) & (s - r <= 2 * SWA_WINDOW)
    scale = HEAD_DIM ** -0.5
    qb = q.reshape(bsz, nb, SWA_BLOCK, SWA_KV_HEADS, SWA_GROUP, HEAD_DIM).swapaxes(0, 1)

    def block(args):
        i, qi = args
        start = i * SWA_BLOCK
        ki = lax.dynamic_slice_in_dim(kp, start, span, axis=1)
        vi = lax.dynamic_slice_in_dim(vp, start, span, axis=1)
        key_pos = start - SWA_WINDOW + jnp.arange(span)
        mask = band & ((key_pos >= 0) & (key_pos < L))[None, :]
        s_lat = jnp.where(mask, jnp.einsum('bqkgd,bskd->bkgqs', qi, ki).astype(jnp.float32) * scale, MASK_VALUE)
        s_ctx = jnp.einsum('bqkgd,bckd->bkgqc', qi, kc).astype(jnp.float32) * scale
        p = sink_softmax(jnp.concatenate([s_lat, s_ctx], axis=-1), sink).astype(v.dtype)
        return (jnp.einsum('bkgqs,bskd->bqkgd', p[..., :span], vi)
                + jnp.einsum('bkgqc,bckd->bqkgd', p[..., span:], vc))

    out = lax.map(block, (jnp.arange(nb), qb))
    return out.swapaxes(0, 1).reshape(bsz, L, SWA_HEADS * HEAD_DIM)


def swa_context(qc, kc, vc, sink):
    bsz, C = qc.shape[:2]
    s = jnp.einsum('bqkgd,bckd->bkgqc', qc, kc).astype(jnp.float32) * HEAD_DIM ** -0.5
    p = sink_softmax(s, sink).astype(vc.dtype)
    return jnp.einsum('bkgqc,bckd->bqkgd', p, vc).reshape(bsz, C, SWA_HEADS * HEAD_DIM)


def mla_queries(cq, q_g, w_uq, w_uk, rope):
    q = (rms_norm(cq, q_g) @ w_uq).reshape(cq.shape[:-1] + (MLA_HEADS, MLA_NOPE + MLA_ROPE))
    q_nope, q_rope = q[..., :MLA_NOPE], q[..., MLA_NOPE:]
    if rope is not None:
        q_rope = apply_rope(q_rope, *rope)
    return jnp.einsum('blhn,rhn->blhr', q_nope, w_uk), q_rope


def mla_attend(q_abs, q_rope, ckv, kr, w_uv):
    s = (jnp.einsum('bqhr,bsr->bhqs', q_abs, ckv)
         + jnp.einsum('bqhe,bse->bhqs', q_rope, kr)).astype(jnp.float32) * (MLA_NOPE + MLA_ROPE) ** -0.5
    p = jax.nn.softmax(s, axis=-1).astype(ckv.dtype)
    o = jnp.einsum('bqhr,rhv->bqhv', jnp.einsum('bhqs,bsr->bqhr', p, ckv), w_uv)
    return o.reshape(o.shape[:2] + (MLA_HEADS * MLA_V,))


def mla_latent(q_abs, q_rope, ckv, kr, w_uv):
    bsz, L = q_abs.shape[:2]
    nb = L // MLA_BLOCK
    qa = q_abs.reshape(bsz, nb, MLA_BLOCK, MLA_HEADS, MLA_KV_RANK).swapaxes(0, 1)
    qr = q_rope.reshape(bsz, nb, MLA_BLOCK, MLA_HEADS, MLA_ROPE).swapaxes(0, 1)
    out = lax.map(lambda a: mla_attend(a[0], a[1], ckv, kr, w_uv), (qa, qr))
    return out.swapaxes(0, 1).reshape(bsz, L, MLA_HEADS * MLA_V)


def merge_branches(h, ys, w_branch, w_gate, b_gate, w_out):
    merged = None
    for i, y in enumerate(ys):
        term = jax.nn.sigmoid(h @ w_gate[i] + b_gate[i]) * (y @ w_branch[i])
        merged = term if merged is None else merged + term
    return merged @ w_out


def token_mixer(h_lat, h_ctx, rope_swa, rope_mla, w_in, conv_w, conv_b, conv_ln_g, conv_ln_b, swa_sink,
                pool_w, pool_scale, mla_q_g, mla_w_uq, mla_kv_g, mla_w_uk, mla_w_uv,
                w_branch, w_gate, b_gate, w_out, ctx_out):
    bsz, L, _ = h_lat.shape
    C = h_ctx.shape[1]
    a_l, q_l, k_l, v_l, p_l, cq_l, ckv_l, kr_l = jnp.split(h_lat @ w_in, IN_SPLITS, axis=-1)
    a_c, q_c, k_c, v_c, p_c, cq_c, ckv_c, kr_c = jnp.split(h_ctx @ w_in, IN_SPLITS, axis=-1)
    kc = k_c.reshape(bsz, C, SWA_KV_HEADS, HEAD_DIM)
    vc = v_c.reshape(bsz, C, SWA_KV_HEADS, HEAD_DIM)
    ckv_ctx = rms_norm(ckv_c, mla_kv_g)

    y_a = conv_module(a_l, conv_w, conv_b, conv_ln_g, conv_ln_b)
    q = apply_rope(q_l.reshape(bsz, L, SWA_HEADS, HEAD_DIM), *rope_swa)
    q = q.reshape(bsz, L, SWA_KV_HEADS, SWA_GROUP, HEAD_DIM)
    k = apply_rope(k_l.reshape(bsz, L, SWA_KV_HEADS, HEAD_DIM), *rope_swa)
    v = v_l.reshape(bsz, L, SWA_KV_HEADS, HEAD_DIM)
    y_b = swa_latent(q, k, v, kc, vc, swa_sink)
    y_c = pool_mixer(p_l, pool_w, pool_scale)
    qa, qr = mla_queries(cq_l, mla_q_g, mla_w_uq, mla_w_uk, rope_mla)
    kr_lat = apply_rope(kr_l[:, :, None, :], *rope_mla)[:, :, 0]
    ckv_keys = jnp.concatenate([rms_norm(ckv_l, mla_kv_g), ckv_ctx], axis=1)
    kr_keys = jnp.concatenate([kr_lat, kr_c], axis=1)
    y_d = mla_latent(qa, qr, ckv_keys, kr_keys, mla_w_uv)
    y_lat = merge_branches(h_lat, (y_a, y_b, y_c, y_d), w_branch, w_gate, b_gate, w_out)
    if not ctx_out:
        return y_lat, None

    y_a_c = conv_module(a_c, conv_w, conv_b, conv_ln_g, conv_ln_b)
    y_b_c = swa_context(q_c.reshape(bsz, C, SWA_KV_HEADS, SWA_GROUP, HEAD_DIM), kc, vc, swa_sink)
    y_c_c = pool_mixer(p_c, pool_w, pool_scale)
    qa_c, qr_c = mla_queries(cq_c, mla_q_g, mla_w_uq, mla_w_uk, None)
    y_d_c = mla_attend(qa_c, qr_c, ckv_ctx, kr_c, mla_w_uv)
    y_ctx = merge_branches(h_ctx, (y_a_c, y_b_c, y_c_c, y_d_c), w_branch, w_gate, b_gate, w_out)
    return y_lat, y_ctx


def moe(h, router_w, router_b, w_gu, b_gu, w_down, b_down):
    n_tok = h.shape[0]
    logits = (h @ router_w + router_b).astype(jnp.float32)
    top_v, top_e = lax.top_k(logits, TOP_K)
    top_w = jax.nn.softmax(top_v, axis=-1)
    n_assign = n_tok * TOP_K
    flat_e = top_e.reshape(-1)
    order = jnp.argsort(flat_e)
    sorted_e = flat_e[order]
    sorted_tok = (order // TOP_K).astype(jnp.int32)
    sorted_w = top_w.reshape(-1)[order]
    counts = jnp.bincount(flat_e, length=N_EXPERTS)
    padded = (counts + EXPERT_BLOCK - 1) // EXPERT_BLOCK * EXPERT_BLOCK
    pad_end = jnp.cumsum(padded)
    pad_start = pad_end - padded
    grp_start = jnp.cumsum(counts) - counts
    dest = pad_start[sorted_e] + jnp.arange(n_assign, dtype=jnp.int32) - grp_start[sorted_e]
    n_blocks = -(-(n_assign + N_EXPERTS * (EXPERT_BLOCK - 1)) // EXPERT_BLOCK)
    n_rows = n_blocks * EXPERT_BLOCK
    row_tok = jnp.full((n_rows,), n_tok, jnp.int32).at[dest].set(sorted_tok)
    row_w = jnp.zeros((n_rows,), jnp.float32).at[dest].set(sorted_w)
    blk_e = jnp.minimum(jnp.searchsorted(pad_end, jnp.arange(n_blocks, dtype=jnp.int32) * EXPERT_BLOCK,
                                         side='right'), N_EXPERTS - 1)
    h_pad = jnp.concatenate([h, jnp.zeros((1, h.shape[1]), h.dtype)], axis=0)

    def step(acc, blk):
        tok, wgt, e = blk
        z = h_pad[tok] @ w_gu[e] + b_gu[e]
        gate = jnp.minimum(z[:, :D_EXPERT], SWIGLU_LIMIT)
        up = jnp.clip(z[:, D_EXPERT:], -SWIGLU_LIMIT, SWIGLU_LIMIT)
        y = ((up + 1) * gate * jax.nn.sigmoid(SWIGLU_ALPHA * gate)) @ w_down[e] + b_down[e]
        return acc.at[tok].add(wgt[:, None].astype(y.dtype) * y), None

    acc, _ = lax.scan(step, jnp.zeros((n_tok + 1, h.shape[1]), h.dtype),
                      (row_tok.reshape(n_blocks, EXPERT_BLOCK), row_w.reshape(n_blocks, EXPERT_BLOCK), blk_e))
    return acc[:n_tok]


def setup_inputs(seed: int = 0) -> dict:
    key = jax.random.key(seed)
    ks = iter(jax.random.split(key, 40))

    def nrm(shape, scale):
        return jax.random.normal(next(ks), shape, jnp.float32) * scale

    def gain(shape):
        return 1.0 + nrm(shape, 0.02)

    L, D = DEPTH, D_MODEL
    return {
        'x': nrm((BATCH, SEQ, D), 1.0),
        'c': nrm((BATCH, D), 1.0),
        'ctx': nrm((BATCH, CTX_LEN, D), 1.0),
        'c_ctx': nrm((D,), 1.0),
        'w_ada': nrm((L, D, 6 * D), D ** -0.5),
        'b_ada': nrm((L, 6 * D), 0.02),
        'w_in': nrm((L, D, IN_W), D ** -0.5),
        'conv_w': nrm((L, CONV_K, CONV_CH), CONV_K ** -0.5),
        'conv_b': nrm((L, CONV_CH), 0.02),
        'conv_ln_g': gain((L, CONV_CH)),
        'conv_ln_b': nrm((L, CONV_CH), 0.02),
        'swa_sink': nrm((L, SWA_HEADS), 1.0),
        'pool_w': nrm((L, len(POOL_WINDOWS), POOL_GROUP, POOL_GROUP), POOL_GROUP ** -0.5),
        'pool_scale': gain((L, POOL_CH)),
        'mla_q_g': gain((L, MLA_Q_RANK)),
        'mla_w_uq': nrm((L, MLA_Q_RANK, MLA_HEADS * (MLA_NOPE + MLA_ROPE)), MLA_Q_RANK ** -0.5),
        'mla_kv_g': gain((L, MLA_KV_RANK)),
        'mla_w_uk': nrm((L, MLA_KV_RANK, MLA_HEADS, MLA_NOPE), MLA_KV_RANK ** -0.5),
        'mla_w_uv': nrm((L, MLA_KV_RANK, MLA_HEADS, MLA_V), MLA_KV_RANK ** -0.5),
        'w_branch': nrm((L, N_BRANCH, BRANCH_W, D), BRANCH_W ** -0.5),
        'w_gate': nrm((L, N_BRANCH, D, D), D ** -0.5),
        'b_gate': nrm((L, N_BRANCH, D), 0.02),
        'w_out': nrm((L, D, D), D ** -0.5 * DN_BETA),
        'ln1_g': gain((L, D)),
        'ln1_b': nrm((L, D), 0.02),
        'router_w': nrm((L, D, N_EXPERTS), D ** -0.5),
        'router_b': nrm((L, N_EXPERTS), 0.01),
        'w_gu': nrm((L, N_EXPERTS, D, 2 * D_EXPERT), D ** -0.5),
        'b_gu': nrm((L, N_EXPERTS, 2 * D_EXPERT), 0.02),
        'w_down': nrm((L, N_EXPERTS, D_EXPERT, D), D_EXPERT ** -0.5 * DN_BETA),
        'b_down': nrm((L, N_EXPERTS, D), 0.02),
        'ln2_g': gain((L, D)),
        'ln2_b': nrm((L, D), 0.02),
    }


def reference(x, c, ctx, c_ctx, w_ada, b_ada, w_in, conv_w, conv_b, conv_ln_g, conv_ln_b, swa_sink,
              pool_w, pool_scale, mla_q_g, mla_w_uq, mla_kv_g, mla_w_uk, mla_w_uv, w_branch, w_gate, b_gate,
              w_out, ln1_g, ln1_b, router_w, router_b, w_gu, b_gu, w_down, b_down, ln2_g, ln2_b):
    seq_len = x.shape[1]
    rope_swa = axial_rope(seq_len, HEAD_DIM)
    rope_mla = axial_rope(seq_len, MLA_ROPE)
    for l in range(DEPTH):
        ctx_out = l < DEPTH - 1
        sh1, sc1, g1, sh2, sc2, g2 = ada_terms(c, w_ada[l], b_ada[l])
        csh1, csc1, cg1, csh2, csc2, cg2 = ada_terms(c_ctx, w_ada[l], b_ada[l])
        y_lat, y_ctx = token_mixer(modulate(x, sh1, sc1), modulate(ctx, csh1, csc1), rope_swa, rope_mla,
                                   w_in[l], conv_w[l], conv_b[l], conv_ln_g[l], conv_ln_b[l], swa_sink[l],
                                   pool_w[l], pool_scale[l], mla_q_g[l], mla_w_uq[l], mla_kv_g[l],
                                   mla_w_uk[l], mla_w_uv[l], w_branch[l], w_gate[l], b_gate[l], w_out[l],
                                   ctx_out)
        x = layer_norm(DN_ALPHA * x + g1 * y_lat, ln1_g[l], ln1_b[l])
        h_lat = modulate(x, sh2, sc2).reshape(-1, D_MODEL)
        n_lat = h_lat.shape[0]
        if ctx_out:
            ctx = layer_norm(DN_ALPHA * ctx + cg1 * y_ctx, ln1_g[l], ln1_b[l])
            h_ctx = modulate(ctx, csh2, csc2).reshape(-1, D_MODEL)
            m = moe(jnp.concatenate([h_lat, h_ctx], axis=0), router_w[l], router_b[l],
                    w_gu[l], b_gu[l], w_down[l], b_down[l])
            m_lat = m[:n_lat]
            ctx = layer_norm(DN_ALPHA * ctx + cg2 * m[n_lat:].reshape(ctx.shape), ln2_g[l], ln2_b[l])
        else:
            m_lat = moe(h_lat, router_w[l], router_b[l], w_gu[l], b_gu[l], w_down[l], b_down[l])
        x = layer_norm(DN_ALPHA * x + g2 * m_lat.reshape(x.shape), ln2_g[l], ln2_b[l])
    return x
```

```python
import functools

import jax
import jax.numpy as jnp
import numpy as np
from jax import lax
from jax.experimental import pallas as pl
from jax.experimental.pallas import tpu as pltpu

F32 = jnp.float32
BF16 = jnp.bfloat16
HIGHEST = lax.Precision.HIGHEST

GRID_W = 64
HEAD_DIM = 64
ROPE_THETA = 10000.0
LN_EPS = 1e-6
MASK_VALUE = -1e30
CONV_CH = 512
CONV_K = 31
SWA_HEADS = 8
SWA_WINDOW = 128
SWA_BLOCK = 128
POOL_CH = 512
POOL_WINDOWS = (2, 4, 8, 16)
POOL_GROUP = 128
MLA_HEADS = 8
MLA_Q_RANK = 256
MLA_KV_RANK = 128
MLA_NOPE = 64
MLA_ROPE = 32
MLA_V = 64
MLA_BLOCK = 128
N_BRANCH = 4
BRANCH_W = 512
N_EXPERTS = 32
TOP_K = 4
D_EXPERT = 1024
SWIGLU_LIMIT = 7.0
SWIGLU_ALPHA = 1.702

LANES = 128
SUBLANES = 8
ROW_TILES = 8
EXPERT_BM = 512
VMEM_LIMIT = 56 * 1024 * 1024


def _cparams(sem):
    return pltpu.CompilerParams(dimension_semantics=sem, vmem_limit_bytes=VMEM_LIMIT)


def _ln(x):
    mu = jnp.mean(x, axis=-1, keepdims=True)
    xc = x - mu
    var = jnp.mean(xc * xc, axis=-1, keepdims=True)
    return xc * lax.rsqrt(var + LN_EPS)


def _sigmoid(x):
    return 1.0 / (1.0 + jnp.exp(-x))


def _ada_kernel(c_ref, w_ref, b_ref, o_ref):
    c = c_ref[...]
    s = c * _sigmoid(c)
    o_ref[...] = jnp.dot(s, w_ref[...], precision=HIGHEST, preferred_element_type=F32) + b_ref[...]


def _ada_terms(cvec, w_ada, b_ada):
    depth, d, n = w_ada.shape
    ra = cvec.shape[0]
    tn = 1536
    return pl.pallas_call(
        _ada_kernel,
        grid=(depth, n // tn),
        in_specs=[
            pl.BlockSpec((ra, d), lambda l, j: (0, 0)),
            pl.BlockSpec((None, d, tn), lambda l, j: (l, 0, j)),
            pl.BlockSpec((None, 1, tn), lambda l, j: (l, 0, j)),
        ],
        out_specs=pl.BlockSpec((None, ra, tn), lambda l, j: (l, 0, j)),
        out_shape=jax.ShapeDtypeStruct((depth, ra, n), F32),
        compiler_params=_cparams(("parallel", "parallel")),
        name="ada_terms",
    )(cvec, w_ada, b_ada.reshape(depth, 1, n))


IN_COLS = 1024 + 512 + 128 + 128 + 512 + 256 + 128 + 128


def _in_kernel(x_ref, sh_ref, sc_ref, w_ref, cs_ref, sn_ref, ck_ref, sk_ref, kvg_ref,
               a_ref, q_ref, k2_ref, v2_ref, p_ref, cq_ref, kc_ref):
    tm = x_ref.shape[0]
    h = _ln(x_ref[...]) * (1.0 + sc_ref[...]) + sh_ref[...]
    z = jnp.dot(h.astype(BF16), w_ref[...], preferred_element_type=F32)
    a_ref[...] = z[:, 0:1024].astype(BF16)
    cs, sn = cs_ref[...], sn_ref[...]
    lane = lax.broadcasted_iota(jnp.int32, (tm, LANES), 1)
    first32 = (lane % 64) < 32

    def rope64(t):
        rot = jnp.where(first32, pltpu.roll(t, LANES - 32, 1), pltpu.roll(t, 32, 1))
        return t * cs + rot * sn

    for j in range(4):
        qj = rope64(z[:, 1024 + LANES * j:1024 + LANES * (j + 1)]) * (HEAD_DIM ** -0.5)
        q_ref[:, LANES * j:LANES * (j + 1)] = qj.astype(BF16)
    k = rope64(z[:, 1536:1664])
    k2_ref[:, 0:LANES] = k.astype(BF16)
    k2_ref[:, LANES:2 * LANES] = pltpu.roll(k, 64, 1).astype(BF16)
    v = z[:, 1664:1792]
    v2_ref[:, 0:LANES] = v.astype(BF16)
    v2_ref[:, LANES:2 * LANES] = pltpu.roll(v, 64, 1).astype(BF16)
    p_ref[...] = z[:, 1792:2304].astype(BF16)
    cq_ref[...] = z[:, 2304:2560].astype(BF16)
    ckv = z[:, 2560:2688]
    ckvn = ckv * lax.rsqrt(jnp.mean(ckv * ckv, axis=-1, keepdims=True) + LN_EPS) * kvg_ref[...]
    kr4 = z[:, 2688:2816]
    first16 = (lane % 32) < 16
    rot = jnp.where(first16, pltpu.roll(kr4, LANES - 16, 1), pltpu.roll(kr4, 16, 1))
    kr4 = kr4 * ck_ref[...] + rot * sk_ref[...]
    kc_ref[:, 0:LANES] = ckvn.astype(BF16)
    kc_ref[:, LANES:2 * LANES] = kr4.astype(BF16)


def _in_proj(x, sh, sc, w_all, tabs, kvg, geo):
    tm, n_tiles, n_lat_tiles, tps, nb = geo["tm"], geo["n_tiles"], geo["n_lat_tiles"], geo["tps"], geo["B"]
    t_rows, d = n_tiles * tm, x.shape[1]

    def cond(i):
        return jnp.where(i < n_lat_tiles, i // tps, nb)

    def pos(i):
        return jnp.where(i < n_lat_tiles, i % tps, tps)

    row = lambda w: pl.BlockSpec((tm, w), lambda i: (i, 0))
    mod = pl.BlockSpec((None, 1, d), lambda i: (cond(i), 0, 0))
    tab = pl.BlockSpec((tm, LANES), lambda i: (pos(i), 0))
    widths = (1024, 512, 256, 256, 512, 256, 256)
    return pl.pallas_call(
        _in_kernel,
        grid=(n_tiles,),
        in_specs=[row(d), mod, mod,
                  pl.BlockSpec((d, IN_COLS), lambda i: (0, 0)),
                  tab, tab, tab, tab,
                  pl.BlockSpec((1, LANES), lambda i: (0, 0))],
        out_specs=[row(w) for w in widths],
        out_shape=[jax.ShapeDtypeStruct((t_rows, w), BF16) for w in widths],
        compiler_params=_cparams(("parallel",)),
        name="in_proj",
    )(x, sh, sc, w_all, tabs["cs"], tabs["sn"], tabs["ck"], tabs["sk"], kvg)


CONV_ROWS = 32
PAD_ROWS = 16


def _local_kernel(a_ref, p_ref, cw_ref, cb_ref, lg_ref, lb_ref, pw_ref, ps_ref, ya_ref, yc_ref, vpad, upad,
                  *, ls):
    ch = min(256, ls)
    zeros = jnp.zeros((PAD_ROWS, CONV_CH), F32)
    vpad[0:PAD_ROWS, :] = zeros
    vpad[ls + PAD_ROWS:ls + 2 * PAD_ROWS, :] = zeros
    upad[0:PAD_ROWS, :] = zeros
    upad[ls + PAD_ROWS:ls + 2 * PAD_ROWS, :] = zeros

    def fill(c, carry):
        r0 = pl.multiple_of(c * ch, ch)
        a = a_ref[pl.ds(r0, ch), :].astype(F32)
        vpad[pl.ds(r0 + PAD_ROWS, ch), :] = a[:, :CONV_CH] * _sigmoid(a[:, CONV_CH:])
        upad[pl.ds(r0 + PAD_ROWS, ch), :] = p_ref[pl.ds(r0, ch), :].astype(F32)
        return carry

    lax.fori_loop(0, ls // ch, fill, 0)

    cw = cw_ref[...]
    cb, lg, lb = cb_ref[...], lg_ref[...], lb_ref[...]

    def conv(c, carry):
        r0 = pl.multiple_of(c * CONV_ROWS, CONV_ROWS)
        win = vpad[pl.ds(r0, CONV_ROWS + 2 * PAD_ROWS), :]
        acc = jnp.zeros((CONV_ROWS, CONV_CH), F32)
        for r in range(SUBLANES):
            wr = win[r:r + CONV_ROWS + 24, :]
            for a in range(4):
                k = SUBLANES * a + r - 1
                if 0 <= k < CONV_K:
                    acc = acc + wr[SUBLANES * a:SUBLANES * a + CONV_ROWS, :] * cw[k:k + 1, :]
        y = _ln(acc + cb) * lg + lb
        ya_ref[pl.ds(r0, CONV_ROWS), :] = (y * _sigmoid(y)).astype(BF16)
        return carry

    lax.fori_loop(0, ls // CONV_ROWS, conv, 0)

    ps = ps_ref[...]

    def pool(c, carry):
        r0 = pl.multiple_of(c * ch, ch)
        t = r0 + lax.broadcasted_iota(jnp.int32, (ch, LANES), 0)
        for g, w in enumerate(POOL_WINDOWS):
            lanes = slice(LANES * g, LANES * (g + 1))
            uwin = upad[pl.ds(r0, ch + 2 * PAD_ROWS), lanes]
            s = None
            for o in range(PAD_ROWS - w // 2, PAD_ROWS + w // 2):
                piece = uwin[o:o + ch, :]
                s = piece if s is None else s + piece
            lo = jnp.maximum(t - w // 2, 0)
            hi = jnp.minimum(t + w // 2, ls)
            d = s / (hi - lo).astype(F32) - uwin[PAD_ROWS:PAD_ROWS + ch, :]
            y = jnp.dot(d.astype(BF16), pw_ref[g], preferred_element_type=F32) * ps[:, lanes]
            yc_ref[pl.ds(r0, ch), lanes] = y.astype(BF16)
        return carry

    lax.fori_loop(0, ls // ch, pool, 0)


def _local_mixers(a, p, lw, ls, blk0, nseq):
    seq_in = lambda w: pl.BlockSpec((ls, w), lambda b: (blk0 + b, 0), pipeline_mode=pl.Buffered(1))
    seq_out = pl.BlockSpec((ls, 512), lambda b: (b, 0))
    full = lambda shape: pl.BlockSpec(shape, lambda b: tuple(0 for _ in shape))
    return pl.pallas_call(
        functools.partial(_local_kernel, ls=ls),
        grid=(nseq,),
        in_specs=[seq_in(1024), seq_in(512), full((32, CONV_CH)), full((1, CONV_CH)), full((1, CONV_CH)),
                  full((1, CONV_CH)), full((4, POOL_GROUP, POOL_GROUP)), full((1, POOL_CH))],
        out_specs=[seq_out, seq_out],
        out_shape=[jax.ShapeDtypeStruct((nseq * ls, 512), BF16)] * 2,
        scratch_shapes=[pltpu.VMEM((ls + 2 * PAD_ROWS, CONV_CH), F32),
                        pltpu.VMEM((ls + 2 * PAD_ROWS, POOL_CH), F32)],
        compiler_params=_cparams(("parallel",)),
        name="local_mixers",
    )(a, p, lw["conv_w"], lw["conv_b"], lw["conv_ln_g"], lw["conv_ln_b"], lw["pool_w"], lw["pool_scale"])


def _swa_heads(q, k2, v2, ok, sink_ref, o_ref):
    bq = q.shape[0]
    lane = lax.broadcasted_iota(jnp.int32, (bq, LANES), 1)
    low = lane < 64
    zero = jnp.zeros((), BF16)
    dn = (((1,), (1,)), ((), ()))
    for j in range(SWA_HEADS // 2):
        g = j // 2
        lo_sl = slice(0, LANES) if g == 0 else slice(LANES, 2 * LANES)
        hi_sl = slice(LANES, 2 * LANES) if g == 0 else slice(0, LANES)
        q2 = q[:, LANES * j:LANES * (j + 1)]
        outs = []
        for e in range(2):
            sl = lo_sl if e == 0 else hi_sl
            qm = jnp.where(low if e == 0 else ~low, q2, zero)
            s = lax.dot_general(qm, k2[:, sl], dn, preferred_element_type=F32)
            if ok is not None:
                s = jnp.where(ok, s, MASK_VALUE)
            sk = sink_ref[2 * j + e:2 * j + e + 1, 0:1]
            m = jnp.maximum(jnp.max(s, axis=-1, keepdims=True), sk)
            pexp = jnp.exp(s - m)
            den = jnp.sum(pexp, axis=-1, keepdims=True) + jnp.exp(sk - m)
            o = jnp.dot(pexp.astype(BF16), v2[:, sl], preferred_element_type=F32)
            outs.append(o / den)
        o_ref[:, LANES * j:LANES * (j + 1)] = jnp.where(low, outs[0], outs[1]).astype(BF16)


def _swa_lat_kernel(q_ref, kp_ref, kc_ref, kn_ref, vp_ref, vc_ref, vn_ref, kx_ref, vx_ref, sink_ref,
                    o_ref, *, seq_len):
    i = pl.program_id(1)
    k2 = jnp.concatenate([kp_ref[...], kc_ref[...], kn_ref[...], kx_ref[...]], axis=0)
    v2 = jnp.concatenate([vp_ref[...], vc_ref[...], vn_ref[...], vx_ref[...]], axis=0)
    nk = k2.shape[0]
    r = lax.broadcasted_iota(jnp.int32, (SWA_BLOCK, nk), 0)
    s = lax.broadcasted_iota(jnp.int32, (SWA_BLOCK, nk), 1)
    kpos = (i - 1) * SWA_BLOCK + s
    diff = kpos - (i * SWA_BLOCK + r)
    ok = ((kpos >= 0) & (kpos < seq_len) & (jnp.abs(diff) <= SWA_WINDOW)) | (s >= 3 * SWA_BLOCK)
    _swa_heads(q_ref[...], k2, v2, ok, sink_ref, o_ref)


def _swa_ctx_kernel(q_ref, kx_ref, vx_ref, sink_ref, o_ref):
    _swa_heads(q_ref[...], kx_ref[...], vx_ref[...], None, sink_ref, o_ref)


def _swa(q, k2, v2, sink_b, geo, with_ctx):
    nb, ls, lc = geo["B"], geo["L"], geo["C"]
    nq = ls // SWA_BLOCK
    ctx_blk0 = nb * ls // lc
    qspec = pl.BlockSpec((SWA_BLOCK, 512), lambda b, i: (b * nq + i, 0))
    kv = lambda off: pl.BlockSpec(
        (SWA_BLOCK, 256), lambda b, i: (b * nq + jnp.clip(i + off, 0, nq - 1), 0))
    cx = pl.BlockSpec((lc, 256), lambda b, i: (ctx_blk0 + b, 0))
    sk = pl.BlockSpec((SUBLANES, LANES), lambda b, i: (0, 0))
    yb = pl.pallas_call(
        functools.partial(_swa_lat_kernel, seq_len=ls),
        grid=(nb, nq),
        in_specs=[qspec, kv(-1), kv(0), kv(1), kv(-1), kv(0), kv(1), cx, cx, sk],
        out_specs=qspec,
        out_shape=jax.ShapeDtypeStruct((nb * ls, 512), BF16),
        compiler_params=_cparams(("parallel", "parallel")),
        name="swa_latent",
    )(q, k2, k2, k2, v2, v2, v2, k2, v2, sink_b)
    if not with_ctx:
        return yb, None
    cx1 = pl.BlockSpec((lc, 256), lambda b: (ctx_blk0 + b, 0))
    yb_ctx = pl.pallas_call(
        _swa_ctx_kernel,
        grid=(nb,),
        in_specs=[pl.BlockSpec((lc, 512), lambda b: (ctx_blk0 + b, 0)), cx1, cx1,
                  pl.BlockSpec((SUBLANES, LANES), lambda b: (0, 0))],
        out_specs=pl.BlockSpec((lc, 512), lambda b: (b, 0)),
        out_shape=jax.ShapeDtypeStruct((nb * lc, 512), BF16),
        compiler_params=_cparams(("parallel",)),
        name="swa_context",
    )(q, k2, v2, sink_b)
    return yb, yb_ctx


MLA_KC = 512
MLA_ROWS = MLA_HEADS * MLA_BLOCK


def _mla_kernel(cq_ref, *rest, n_lat_chunks, kc):
    if n_lat_chunks:
        kl_ref, rest = rest[0], rest[1:]
    (kx_ref, qg_ref, wuq_ref, wukp_ref, wuvp_ref, cs_ref, sn_ref, o_ref, q_sc, m_sc, l_sc, acc_sc) = rest
    scale = (MLA_NOPE + MLA_ROPE) ** -0.5
    cq = cq_ref[...].astype(F32)
    cqn = cq * lax.rsqrt(jnp.mean(cq * cq, axis=-1, keepdims=True) + LN_EPS) * qg_ref[...]
    q = jnp.dot(cqn.astype(BF16), wuq_ref[...], preferred_element_type=F32)
    qr = q[:, 512:768]
    lane2 = lax.broadcasted_iota(jnp.int32, (MLA_BLOCK, 2 * LANES), 1)
    rot = jnp.where((lane2 % 32) < 16, pltpu.roll(qr, 2 * LANES - 16, 1), pltpu.roll(qr, 16, 1))
    qr = (qr * cs_ref[...] + rot * sn_ref[...]) * scale
    slot_of_lane = lax.broadcasted_iota(jnp.int32, (MLA_BLOCK, LANES), 1) // 32
    for j in range(MLA_HEADS // 2):
        qa2 = jnp.dot(q[:, LANES * j:LANES * (j + 1)].astype(BF16), wukp_ref[j],
                      preferred_element_type=F32) * scale
        for e in range(2):
            h = 2 * j + e
            rows = slice(h * MLA_BLOCK, (h + 1) * MLA_BLOCK)
            qrh = jnp.where(slot_of_lane == (h % 4), qr[:, LANES * (h // 4):LANES * (h // 4 + 1)], 0.0)
            q_sc[rows, 0:LANES] = qa2[:, LANES * e:LANES * (e + 1)].astype(BF16)
            q_sc[rows, LANES:2 * LANES] = qrh.astype(BF16)

    m_sc[...] = jnp.full(m_sc.shape, MASK_VALUE, F32)
    l_sc[...] = jnp.zeros(l_sc.shape, F32)
    acc_sc[...] = jnp.zeros(acc_sc.shape, F32)
    dn = (((1,), (1,)), ((), ()))

    def attend(keys):
        s = lax.dot_general(q_sc[...], keys, dn, preferred_element_type=F32)
        m_old = m_sc[...]
        m_new = jnp.maximum(m_old, jnp.max(s, axis=-1, keepdims=True))
        alpha = jnp.exp(m_old - m_new)
        pexp = jnp.exp(s - m_new)
        l_sc[...] = alpha * l_sc[...] + jnp.sum(pexp, axis=-1, keepdims=True)
        acc_sc[...] = alpha * acc_sc[...] + jnp.dot(pexp.astype(BF16), keys[:, 0:LANES],
                                                    preferred_element_type=F32)
        m_sc[...] = m_new

    if n_lat_chunks:
        def chunk(c, carry):
            attend(kl_ref[pl.ds(pl.multiple_of(c * kc, kc), kc), :])
            return carry
        lax.fori_loop(0, n_lat_chunks, chunk, 0)
    attend(kx_ref[...])

    o = acc_sc[...] / l_sc[...]
    for j in range(MLA_HEADS // 2):
        o2 = jnp.concatenate([o[(2 * j) * MLA_BLOCK:(2 * j + 1) * MLA_BLOCK, :],
                              o[(2 * j + 1) * MLA_BLOCK:(2 * j + 2) * MLA_BLOCK, :]], axis=1)
        o_ref[:, LANES * j:LANES * (j + 1)] = jnp.dot(
            o2.astype(BF16), wuvp_ref[j], preferred_element_type=F32).astype(BF16)


def _mla(cq, kc_all, lw, tabs, geo, lat):
    nb, ls, lc = geo["B"], geo["L"], geo["C"]
    ctx_blk0 = nb * ls // lc
    if lat:
        nq = ls // MLA_BLOCK
        qmap = lambda b, i: (b * nq + i, 0)
        tmap = lambda b, i: (i, 0)
        kc = min(MLA_KC, ls)
        n_chunks = ls // kc
    else:
        nq = lc // MLA_BLOCK
        qmap = lambda b, i: (nb * (ls // MLA_BLOCK) + b * nq + i, 0)
        tmap = lambda b, i: (ls // MLA_BLOCK, 0)
        kc, n_chunks = 0, 0
    omap = lambda b, i: (b * nq + i, 0)
    full = lambda shape: pl.BlockSpec(shape, lambda b, i: tuple(0 for _ in shape))
    in_specs = [pl.BlockSpec((MLA_BLOCK, 256), qmap)]
    args = [cq]
    if lat:
        in_specs.append(pl.BlockSpec((ls, 256), lambda b, i: (b, 0)))
        args.append(kc_all)
    in_specs += [pl.BlockSpec((lc, 256), lambda b, i: (ctx_blk0 + b, 0)),
                 full((1, 256)), full((256, 768)), full((4, LANES, 256)), full((4, 256, LANES)),
                 pl.BlockSpec((MLA_BLOCK, 256), tmap), pl.BlockSpec((MLA_BLOCK, 256), tmap)]
    args += [kc_all, lw["mla_q_g"], lw["wuq"], lw["wukp"], lw["wuvp"], tabs["cq"], tabs["sq"]]
    return pl.pallas_call(
        functools.partial(_mla_kernel, n_lat_chunks=n_chunks, kc=kc),
        grid=(nb, nq),
        in_specs=in_specs,
        out_specs=pl.BlockSpec((MLA_BLOCK, 512), omap),
        out_shape=jax.ShapeDtypeStruct((nb * nq * MLA_BLOCK, 512), BF16),
        scratch_shapes=[pltpu.VMEM((MLA_ROWS, 256), BF16), pltpu.VMEM((MLA_ROWS, 1), F32),
                        pltpu.VMEM((MLA_ROWS, 1), F32), pltpu.VMEM((MLA_ROWS, LANES), F32)],
        compiler_params=_cparams(("parallel", "parallel")),
        name="mla_latent" if lat else "mla_context",
    )(*args)


def _merge_kernel(x_ref, *rest, alpha, n_lat_tiles, has_ctx):
    n_y = 2 * N_BRANCH if has_ctx else N_BRANCH
    y_refs, rest = rest[:n_y], rest[n_y:]
    (sh1_ref, sc1_ref, g1_ref, sh2_ref, sc2_ref, wg_ref, bg_ref, wb_ref, wo_ref, l1g_ref, l1b_ref,
     rw_ref, rb_ref, x1_ref, h2_ref, ei_ref, ew_ref) = rest
    tm = x_ref.shape[0]
    is_ctx = pl.program_id(0) >= n_lat_tiles
    x = x_ref[...]
    h = (_ln(x) * (1.0 + sc1_ref[...]) + sh1_ref[...]).astype(BF16)
    merged = None
    for i in range(N_BRANCH):
        yi = y_refs[i][...]
        if has_ctx:
            yi = jnp.where(is_ctx, y_refs[N_BRANCH + i][...], yi)
        gate = _sigmoid(jnp.dot(h, wg_ref[i], preferred_element_type=F32) + bg_ref[i])
        term = gate * jnp.dot(yi, wb_ref[i], preferred_element_type=F32)
        merged = term if merged is None else merged + term
    y = jnp.dot(merged.astype(BF16), wo_ref[...], preferred_element_type=F32)
    x1 = _ln(alpha * x + g1_ref[...] * y) * l1g_ref[...] + l1b_ref[...]
    x1_ref[...] = x1
    h2 = _ln(x1) * (1.0 + sc2_ref[...]) + sh2_ref[...]
    for j in range(ROW_TILES):
        h2_ref[pl.ds(j, tm, stride=ROW_TILES), :] = h2[:, LANES * j:LANES * (j + 1)]
    lg = jnp.dot(h2, rw_ref[...], precision=HIGHEST, preferred_element_type=F32) + rb_ref[...]
    lane = lax.broadcasted_iota(jnp.int32, (tm, LANES), 1).astype(F32)
    vals, idxs = [], []
    for _ in range(TOP_K):
        mx = jnp.max(lg, axis=-1, keepdims=True)
        ix = jnp.min(jnp.where(lg == mx, lane, float(LANES)), axis=-1, keepdims=True)
        vals.append(mx)
        idxs.append(ix)
        lg = jnp.where(lane == ix, -3.0e38, lg)
    exps = [jnp.exp(v - vals[0]) for v in vals]
    den = exps[0] + exps[1] + exps[2] + exps[3]
    ei = jnp.zeros((tm, LANES), F32)
    ew = jnp.zeros((tm, LANES), F32)
    for k in range(TOP_K):
        ei = jnp.where(lane == float(k), idxs[k], ei)
        ew = jnp.where(lane == float(k), exps[k] / den, ew)
    ei_ref[...] = ei.astype(jnp.int32)
    ew_ref[...] = ew


def _merge(x, ys_lat, ys_ctx, mods, lw, geo, alpha):
    tm, n_lat_tiles, tps, nb = geo["tm"], geo["n_lat_tiles"], geo["tps"], geo["B"]
    d = x.shape[1]
    has_ctx = ys_ctx is not None
    n_tiles = geo["n_tiles"] if has_ctx else n_lat_tiles
    n_ctx_tiles = geo["n_tiles"] - n_lat_tiles
    t_rows = n_tiles * tm

    def cond(i):
        return jnp.where(i < n_lat_tiles, i // tps, nb)

    row = lambda w: pl.BlockSpec((tm, w), lambda i: (i, 0))
    y_lat = pl.BlockSpec((tm, BRANCH_W), lambda i: (jnp.minimum(i, n_lat_tiles - 1), 0))
    y_ctx = pl.BlockSpec((tm, BRANCH_W), lambda i: (jnp.clip(i - n_lat_tiles, 0, n_ctx_tiles - 1), 0))
    y_specs = [y_lat] * N_BRANCH + ([y_ctx] * N_BRANCH if has_ctx else [])
    ys = tuple(ys_lat) + (tuple(ys_ctx) if has_ctx else ())
    mod = pl.BlockSpec((None, 1, d), lambda i: (cond(i), 0, 0))
    full = lambda shape: pl.BlockSpec(shape, lambda i: tuple(0 for _ in shape), pipeline_mode=pl.Buffered(1))
    return pl.pallas_call(
        functools.partial(_merge_kernel, alpha=alpha, n_lat_tiles=n_lat_tiles, has_ctx=has_ctx),
        grid=(n_tiles,),
        name="merge_router",
        in_specs=[row(d)] + y_specs + [mod, mod, mod, mod, mod,
                  full((4, d, d)), full((4, 1, d)), full((4, BRANCH_W, d)), full((d, d)),
                  full((1, d)), full((1, d)), full((d, LANES)), full((1, LANES))],
        out_specs=[row(d), pl.BlockSpec((tm * ROW_TILES, LANES), lambda i: (i, 0)), row(LANES), row(LANES)],
        out_shape=[jax.ShapeDtypeStruct((t_rows, d), F32),
                   jax.ShapeDtypeStruct((t_rows * ROW_TILES, LANES), F32),
                   jax.ShapeDtypeStruct((t_rows, LANES), jnp.int32),
                   jax.ShapeDtypeStruct((t_rows, LANES), F32)],
        compiler_params=_cparams(("parallel",)),
    )(x, *ys, mods["sh1"], mods["sc1"], mods["g1"], mods["sh2"], mods["sc2"],
      lw["w_gate"], lw["b_gate"], lw["w_branch"], lw["w_out"], lw["ln1_g"], lw["ln1_b"],
      lw["router_w"], lw["router_b"])


GATHER_UNROLL = 8


def _gmm_kernel(blk_e_ref, nvalid_ref, idx_ref, h2_hbm, wgu_ref, bgu_ref, wdn_ref, bdn_ref,
                ytk_hbm, idx_sm, xbuf, ybuf, sems):
    del blk_e_ref
    b = pl.program_id(0)
    bm = xbuf.shape[0] // ROW_TILES

    @pl.when(b == 0)
    def _():
        ybuf[...] = jnp.zeros(ybuf.shape, F32)
        pad0 = ytk_hbm.shape[0] - bm * ROW_TILES
        cp = pltpu.make_async_copy(ybuf, ytk_hbm.at[pl.ds(pad0, bm * ROW_TILES), :], sems.at[2])
        cp.start()
        cp.wait()

    @pl.when(b < nvalid_ref[0])
    def _():
        cp = pltpu.make_async_copy(idx_ref.at[0], idx_sm, sems.at[0])
        cp.start()
        cp.wait()

        def gather(o, carry):
            for u in range(GATHER_UNROLL):
                r = o * GATHER_UNROLL + u
                src = pl.multiple_of(idx_sm[0, r] * ROW_TILES, ROW_TILES)
                dst = pl.multiple_of(r * ROW_TILES, ROW_TILES)
                pltpu.make_async_copy(h2_hbm.at[pl.ds(src, ROW_TILES), :],
                                      xbuf.at[pl.ds(dst, ROW_TILES), :], sems.at[1]).start()
            return carry

        lax.fori_loop(0, bm // GATHER_UNROLL, gather, 0)
        pltpu.make_async_copy(h2_hbm.at[pl.ds(0, bm * ROW_TILES), :], xbuf, sems.at[1]).wait()

        xs = jnp.concatenate(
            [xbuf[pl.ds(j, bm, stride=ROW_TILES), :] for j in range(ROW_TILES)], axis=1).astype(BF16)
        z = jnp.dot(xs, wgu_ref[...], preferred_element_type=F32) + bgu_ref[...]
        gate = jnp.minimum(z[:, :D_EXPERT], SWIGLU_LIMIT)
        up = jnp.clip(z[:, D_EXPERT:], -SWIGLU_LIMIT, SWIGLU_LIMIT)
        act = (up + 1.0) * gate * _sigmoid(SWIGLU_ALPHA * gate)
        y = jnp.dot(act.astype(BF16), wdn_ref[...], preferred_element_type=F32) + bdn_ref[...]
        for j in range(ROW_TILES):
            ybuf[pl.ds(j, bm, stride=ROW_TILES), :] = y[:, LANES * j:LANES * (j + 1)]

        def scatter(o, carry):
            for u in range(GATHER_UNROLL):
                r = o * GATHER_UNROLL + u
                src = pl.multiple_of(r * ROW_TILES, ROW_TILES)
                dst = pl.multiple_of(idx_sm[1, r] * ROW_TILES, ROW_TILES)
                pltpu.make_async_copy(ybuf.at[pl.ds(src, ROW_TILES), :],
                                      ytk_hbm.at[pl.ds(dst, ROW_TILES), :], sems.at[2]).start()
            return carry

        lax.fori_loop(0, bm // GATHER_UNROLL, scatter, 0)
        pltpu.make_async_copy(ybuf, ytk_hbm.at[pl.ds(0, bm * ROW_TILES), :], sems.at[2]).wait()


def _gmm(h2g, plan, lw, n_tok):
    bm = EXPERT_BM
    n_blocks = plan["idx"].shape[0]
    d = lw["w_gu"].shape[1]
    n_slots = n_tok * TOP_K + bm
    grid_spec = pltpu.PrefetchScalarGridSpec(
        num_scalar_prefetch=2,
        grid=(n_blocks,),
        in_specs=[
            pl.BlockSpec((1, 2, bm), lambda b, be, nv: (b, 0, 0)),
            pl.BlockSpec(memory_space=pl.ANY),
            pl.BlockSpec((None, d, 2 * D_EXPERT), lambda b, be, nv: (be[b], 0, 0)),
            pl.BlockSpec((None, 1, 2 * D_EXPERT), lambda b, be, nv: (be[b], 0, 0)),
            pl.BlockSpec((None, D_EXPERT, d), lambda b, be, nv: (be[b], 0, 0)),
            pl.BlockSpec((None, 1, d), lambda b, be, nv: (be[b], 0, 0)),
        ],
        out_specs=pl.BlockSpec(memory_space=pl.ANY),
        scratch_shapes=[pltpu.SMEM((2, bm), jnp.int32),
                        pltpu.VMEM((bm * ROW_TILES, LANES), F32),
                        pltpu.VMEM((bm * ROW_TILES, LANES), F32),
                        pltpu.SemaphoreType.DMA((3,))],
    )
    return pl.pallas_call(
        _gmm_kernel,
        grid_spec=grid_spec,
        out_shape=jax.ShapeDtypeStruct((n_slots * ROW_TILES, LANES), F32),
        compiler_params=_cparams(("arbitrary",)),
        name="expert_gmm",
    )(plan["blk_e"], plan["nvalid"], plan["idx"], h2g, lw["w_gu"], lw["b_gu"], lw["w_down"], lw["b_down"])


def _plan(ei, n_tok):
    bm = EXPERT_BM
    n_assign = n_tok * TOP_K
    flat_e = ei.reshape(-1)
    order = jnp.argsort(flat_e, stable=True).astype(jnp.int32)
    sorted_e = flat_e[order]
    counts = jnp.sum((flat_e[:, None] == jnp.arange(N_EXPERTS, dtype=jnp.int32)[None, :]).astype(jnp.int32), axis=0)
    padded = (counts + bm - 1) // bm * bm
    pad_end = jnp.cumsum(padded)
    pad_start = pad_end - padded
    grp_start = jnp.cumsum(counts) - counts
    dest = pad_start[sorted_e] + jnp.arange(n_assign, dtype=jnp.int32) - grp_start[sorted_e]
    n_blocks = -(-(n_assign + N_EXPERTS * (bm - 1)) // bm)
    n_rows = n_blocks * bm
    row_asg = jnp.full((n_rows,), -1, jnp.int32).at[dest].set(order)
    valid = row_asg >= 0
    row_src = jnp.where(valid, row_asg // TOP_K, 0)
    row_dst = jnp.where(valid, row_asg, n_assign + jnp.arange(n_rows, dtype=jnp.int32) % bm)
    idx = jnp.stack([row_src.reshape(n_blocks, bm), row_dst.reshape(n_blocks, bm)], axis=1)
    blk_e = jnp.minimum(jnp.searchsorted(pad_end, jnp.arange(n_blocks, dtype=jnp.int32) * bm, side="right"),
                        N_EXPERTS - 1).astype(jnp.int32)
    nvalid = (pad_end[-1:] // bm).astype(jnp.int32)
    return {"idx": idx.astype(jnp.int32), "blk_e": blk_e, "nvalid": nvalid}


def _combine_kernel(x_ref, ytk_ref, ew_ref, g2_ref, lg_ref, lb_ref, o_ref, *, alpha):
    tm = x_ref.shape[0]
    ew = ew_ref[...]
    stride = TOP_K * ROW_TILES
    cols = []
    for j in range(ROW_TILES):
        acc = None
        for k in range(TOP_K):
            piece = ytk_ref[pl.ds(k * ROW_TILES + j, tm, stride=stride), :] * ew[:, k:k + 1]
            acc = piece if acc is None else acc + piece
        cols.append(acc)
    m = jnp.concatenate(cols, axis=1)
    o_ref[...] = _ln(alpha * x_ref[...] + g2_ref[...] * m) * lg_ref[...] + lb_ref[...]


def _combine(x1, ytk, ew, g2, lw, geo, n_tiles, alpha):
    tm, n_lat_tiles, tps, nb = geo["tmc"], geo["n_lat_tiles"], geo["tps"], geo["B"]
    ratio = geo["tm"] // tm
    d = x1.shape[1]

    def cond(i):
        return jnp.where(i < n_lat_tiles * ratio, i // (tps * ratio), nb)

    row = lambda w: pl.BlockSpec((tm, w), lambda i: (i, 0))
    full = lambda shape: pl.BlockSpec(shape, lambda i: tuple(0 for _ in shape))
    return pl.pallas_call(
        functools.partial(_combine_kernel, alpha=alpha),
        grid=(n_tiles * ratio,),
        in_specs=[row(d), pl.BlockSpec((tm * TOP_K * ROW_TILES, LANES), lambda i: (i, 0)), row(LANES),
                  pl.BlockSpec((None, 1, d), lambda i: (cond(i), 0, 0)), full((1, d)), full((1, d))],
        out_specs=row(d),
        out_shape=jax.ShapeDtypeStruct((n_tiles * tm * ratio, d), F32),
        compiler_params=_cparams(("parallel",)),
        name="moe_combine",
    )(x1, ytk, ew, g2, lw["ln2_g"], lw["ln2_b"])


def _rope_tables(seq_len, tm):
    rows = seq_len // GRID_W
    row = jnp.repeat(jnp.arange(rows, dtype=F32), GRID_W)
    col = jnp.tile(jnp.arange(GRID_W, dtype=F32), rows)

    def table(rot_dim, reps, pad_rows):
        n_freq = rot_dim // 4
        inv = ROPE_THETA ** (-jnp.arange(n_freq, dtype=F32) / n_freq)
        ang = jnp.concatenate([row[:, None] * inv, col[:, None] * inv], axis=-1)
        cos, sin = jnp.cos(ang), jnp.sin(ang)
        c = jnp.tile(jnp.concatenate([cos, cos], axis=-1), (1, reps))
        s = jnp.tile(jnp.concatenate([-sin, sin], axis=-1), (1, reps))
        c = jnp.concatenate([c, jnp.ones((pad_rows, c.shape[1]), F32)], axis=0)
        s = jnp.concatenate([s, jnp.zeros((pad_rows, s.shape[1]), F32)], axis=0)
        return c, s

    cs, sn = table(HEAD_DIM, 2, tm)
    ck, sk = table(MLA_ROPE, 4, tm)
    cq, sq = table(MLA_ROPE, 8, MLA_BLOCK)
    return {"cs": cs, "sn": sn, "ck": ck, "sk": sk, "cq": cq, "sq": sq}


def _layer_weights(l, w_in, conv_w, conv_b, conv_ln_g, conv_ln_b, swa_sink, pool_w, pool_scale, mla_q_g,
                   mla_w_uq, mla_kv_g, mla_w_uk, mla_w_uv, w_branch, w_gate, b_gate, w_out, ln1_g, ln1_b,
                   router_w, router_b, w_gu_bf, b_gu, w_down_bf, b_down, ln2_g, ln2_b):
    d = w_in.shape[1]
    wi = w_in[l]
    kr = wi[:, 2688:2720]
    w_all = jnp.concatenate([wi[:, :2688], kr, kr, kr, kr], axis=1).astype(BF16)
    wuq = mla_w_uq[l].reshape(MLA_Q_RANK, MLA_HEADS, MLA_NOPE + MLA_ROPE)
    wuq = jnp.concatenate([wuq[:, :, :MLA_NOPE].reshape(MLA_Q_RANK, -1),
                           wuq[:, :, MLA_NOPE:].reshape(MLA_Q_RANK, -1)], axis=1).astype(BF16)
    wuk = jnp.transpose(mla_w_uk[l], (1, 2, 0))
    wuv = jnp.transpose(mla_w_uv[l], (1, 0, 2))
    zk = jnp.zeros((MLA_NOPE, MLA_KV_RANK), F32)
    zv = jnp.zeros((MLA_KV_RANK, MLA_V), F32)
    wukp = jnp.stack([jnp.block([[wuk[2 * j], zk], [zk, wuk[2 * j + 1]]]) for j in range(MLA_HEADS // 2)])
    wuvp = jnp.stack([jnp.block([[wuv[2 * j], zv], [zv, wuv[2 * j + 1]]]) for j in range(MLA_HEADS // 2)])
    rw = jnp.zeros((d, LANES), F32).at[:, :N_EXPERTS].set(router_w[l])
    rb = jnp.full((1, LANES), MASK_VALUE, F32).at[0, :N_EXPERTS].set(router_b[l])
    return {
        "w_all": w_all,
        "conv_w": jnp.concatenate([conv_w[l], jnp.zeros((1, CONV_CH), F32)], axis=0),
        "conv_b": conv_b[l][None], "conv_ln_g": conv_ln_g[l][None], "conv_ln_b": conv_ln_b[l][None],
        "sink": jnp.broadcast_to(swa_sink[l][:, None], (SWA_HEADS, LANES)),
        "pool_w": pool_w[l].astype(BF16), "pool_scale": pool_scale[l][None],
        "mla_q_g": mla_q_g[l][None], "mla_kv_g": mla_kv_g[l][None],
        "wuq": wuq, "wukp": wukp.astype(BF16), "wuvp": wuvp.astype(BF16),
        "w_branch": w_branch[l].astype(BF16), "w_gate": w_gate[l].astype(BF16),
        "b_gate": b_gate[l][:, None, :], "w_out": w_out[l].astype(BF16),
        "ln1_g": ln1_g[l][None], "ln1_b": ln1_b[l][None],
        "router_w": rw, "router_b": rb,
        "w_gu": w_gu_bf[l], "b_gu": b_gu[l][:, None, :],
        "w_down": w_down_bf[l], "b_down": b_down[l][:, None, :],
        "ln2_g": ln2_g[l][None], "ln2_b": ln2_b[l][None],
    }


def _token_tile(seq_len, ctx_rows):
    for tm in (512, 256, 128):
        if seq_len % tm == 0 and ctx_rows % tm == 0:
            return tm
    raise ValueError("unsupported sequence / context lengths")


def kernel(x, c, ctx, c_ctx, w_ada, b_ada, w_in, conv_w, conv_b, conv_ln_g, conv_ln_b, swa_sink, pool_w,
           pool_scale, mla_q_g, mla_w_uq, mla_kv_g, mla_w_uk, mla_w_uv, w_branch, w_gate, b_gate, w_out,
           ln1_g, ln1_b, router_w, router_b, w_gu, b_gu, w_down, b_down, ln2_g, ln2_b):
    nb, ls, d = x.shape
    lc = ctx.shape[1]
    depth = w_ada.shape[0]
    assert ls % GRID_W == 0 and ls % SWA_BLOCK == 0 and lc % MLA_BLOCK == 0 and (nb * ls) % lc == 0
    tm = _token_tile(ls, nb * lc)
    t_lat, t_ctx = nb * ls, nb * lc
    geo = {"B": nb, "L": ls, "C": lc, "tm": tm, "tmc": min(tm, 256), "tps": ls // tm,
           "n_lat_tiles": t_lat // tm, "n_tiles": (t_lat + t_ctx) // tm}
    alpha = (2 * depth) ** 0.25

    ra = -(-(nb + 1) // SUBLANES) * SUBLANES
    cvec = jnp.zeros((ra, d), F32).at[:nb].set(c).at[nb].set(c_ctx)
    ada = _ada_terms(cvec, w_ada, b_ada).reshape(depth, ra, 6, 1, d)
    tabs = _rope_tables(ls, tm)
    w_gu_bf = w_gu.astype(BF16)
    w_down_bf = w_down.astype(BF16)

    xa = jnp.concatenate([x.reshape(t_lat, d), ctx.reshape(t_ctx, d)], axis=0)
    for l in range(depth):
        ctx_out = l < depth - 1
        n_tiles = geo["n_tiles"] if ctx_out else geo["n_lat_tiles"]
        names = ("sh1", "sc1", "g1", "sh2", "sc2", "g2")
        mods = {n: ada[l, :, i] for i, n in enumerate(names)}
        lw = _layer_weights(l, w_in, conv_w, conv_b, conv_ln_g, conv_ln_b, swa_sink, pool_w, pool_scale,
                            mla_q_g, mla_w_uq, mla_kv_g, mla_w_uk, mla_w_uv, w_branch, w_gate, b_gate,
                            w_out, ln1_g, ln1_b, router_w, router_b, w_gu_bf, b_gu, w_down_bf, b_down,
                            ln2_g, ln2_b)
        a, q, k2, v2, p, cq, kc = _in_proj(xa, mods["sh1"], mods["sc1"], lw["w_all"], tabs,
                                           lw["mla_kv_g"], geo)
        ya, yc = _local_mixers(a, p, lw, ls, 0, nb)
        yb, yb_ctx = _swa(q, k2, v2, lw["sink"], geo, ctx_out)
        yd = _mla(cq, kc, lw, tabs, geo, True)
        ys_ctx = None
        if ctx_out:
            ya_ctx, yc_ctx = _local_mixers(a, p, lw, lc, t_lat // lc, nb)
            yd_ctx = _mla(cq, kc, lw, tabs, geo, False)
            ys_ctx = (ya_ctx, yb_ctx, yc_ctx, yd_ctx)
        x1, h2g, ei, ew = _merge(xa, (ya, yb, yc, yd), ys_ctx, mods, lw, geo, alpha)
        n_tok = n_tiles * tm
        plan = _plan(ei[:, :TOP_K], n_tok)
        ytk = _gmm(h2g, plan, lw, n_tok)
        xa = _combine(x1, ytk, ew, mods["g2"], lw, geo, n_tiles, alpha)
    return xa[:t_lat].reshape(nb, ls, d)
```

```python
import functools

import jax
import jax.numpy as jnp
import numpy as np
from jax import lax
from jax.experimental import pallas as pl
from jax.experimental.pallas import tpu as pltpu

F32 = jnp.float32
BF16 = jnp.bfloat16
HIGHEST = lax.Precision.HIGHEST

GRID_W = 64
HEAD_DIM = 64
ROPE_THETA = 10000.0
LN_EPS = 1e-6
MASK_VALUE = -1e30
CONV_CH = 512
CONV_K = 31
SWA_HEADS = 8
SWA_WINDOW = 128
SWA_BLOCK = 128
POOL_CH = 512
POOL_WINDOWS = (2, 4, 8, 16)
POOL_GROUP = 128
MLA_HEADS = 8
MLA_Q_RANK = 256
MLA_KV_RANK = 128
MLA_NOPE = 64
MLA_ROPE = 32
MLA_V = 64
MLA_BLOCK = 128
N_BRANCH = 4
BRANCH_W = 512
N_EXPERTS = 32
TOP_K = 4
D_EXPERT = 1024
SWIGLU_LIMIT = 7.0
SWIGLU_ALPHA = 1.702

LANES = 128
SUBLANES = 8
ROW_TILES = 8
EXPERT_BM = 512
VMEM_LIMIT = 56 * 1024 * 1024


def _cparams(sem):
    return pltpu.CompilerParams(dimension_semantics=sem, vmem_limit_bytes=VMEM_LIMIT)


def _ln(x):
    mu = jnp.mean(x, axis=-1, keepdims=True)
    xc = x - mu
    var = jnp.mean(xc * xc, axis=-1, keepdims=True)
    return xc * lax.rsqrt(var + LN_EPS)


def _sigmoid(x):
    return 1.0 / (1.0 + jnp.exp(-x))


def _ada_kernel(c_ref, w_ref, b_ref, o_ref):
    c = c_ref[...]
    s = c * _sigmoid(c)
    o_ref[...] = jnp.dot(s, w_ref[...], precision=HIGHEST, preferred_element_type=F32) + b_ref[...]


def _ada_terms(cvec, w_ada, b_ada):
    depth, d, n = w_ada.shape
    ra = cvec.shape[0]
    tn = 1536
    return pl.pallas_call(
        _ada_kernel,
        grid=(depth, n // tn),
        in_specs=[
            pl.BlockSpec((ra, d), lambda l, j: (0, 0)),
            pl.BlockSpec((None, d, tn), lambda l, j: (l, 0, j)),
            pl.BlockSpec((None, 1, tn), lambda l, j: (l, 0, j)),
        ],
        out_specs=pl.BlockSpec((None, ra, tn), lambda l, j: (l, 0, j)),
        out_shape=jax.ShapeDtypeStruct((depth, ra, n), F32),
        compiler_params=_cparams(("parallel", "parallel")),
        name="ada_terms",
    )(cvec, w_ada, b_ada.reshape(depth, 1, n))


IN_COLS = 1024 + 512 + 128 + 128 + 512 + 256 + 128 + 128


def _in_kernel(x_ref, sh_ref, sc_ref, w_ref, cs_ref, sn_ref, ck_ref, sk_ref, kvg_ref,
               a_ref, q_ref, k2_ref, v2_ref, p_ref, cq_ref, kc_ref):
    tm = x_ref.shape[0]
    h = _ln(x_ref[...]) * (1.0 + sc_ref[...]) + sh_ref[...]
    z = jnp.dot(h.astype(BF16), w_ref[...], preferred_element_type=F32)
    a_ref[...] = z[:, 0:1024].astype(BF16)
    cs, sn = cs_ref[...], sn_ref[...]
    lane = lax.broadcasted_iota(jnp.int32, (tm, LANES), 1)
    first32 = (lane % 64) < 32

    def rope64(t):
        rot = jnp.where(first32, pltpu.roll(t, LANES - 32, 1), pltpu.roll(t, 32, 1))
        return t * cs + rot * sn

    for j in range(4):
        qj = rope64(z[:, 1024 + LANES * j:1024 + LANES * (j + 1)]) * (HEAD_DIM ** -0.5)
        q_ref[:, LANES * j:LANES * (j + 1)] = qj.astype(BF16)
    k = rope64(z[:, 1536:1664])
    k2_ref[:, 0:LANES] = k.astype(BF16)
    k2_ref[:, LANES:2 * LANES] = pltpu.roll(k, 64, 1).astype(BF16)
    v = z[:, 1664:1792]
    v2_ref[:, 0:LANES] = v.astype(BF16)
    v2_ref[:, LANES:2 * LANES] = pltpu.roll(v, 64, 1).astype(BF16)
    p_ref[...] = z[:, 1792:2304].astype(BF16)
    cq_ref[...] = z[:, 2304:2560].astype(BF16)
    ckv = z[:, 2560:2688]
    ckvn = ckv * lax.rsqrt(jnp.mean(ckv * ckv, axis=-1, keepdims=True) + LN_EPS) * kvg_ref[...]
    kr4 = z[:, 2688:2816]
    first16 = (lane % 32) < 16
    rot = jnp.where(first16, pltpu.roll(kr4, LANES - 16, 1), pltpu.roll(kr4, 16, 1))
    kr4 = kr4 * ck_ref[...] + rot * sk_ref[...]
    kc_ref[:, 0:LANES] = ckvn.astype(BF16)
    kc_ref[:, LANES:2 * LANES] = kr4.astype(BF16)


def _in_proj(x, sh, sc, w_all, tabs, kvg, geo):
    tm, n_tiles, n_lat_tiles, tps, nb = geo["tm"], geo["n_tiles"], geo["n_lat_tiles"], geo["tps"], geo["B"]
    t_rows, d = n_tiles * tm, x.shape[1]

    def cond(i):
        return jnp.where(i < n_lat_tiles, i // tps, nb)

    def pos(i):
        return jnp.where(i < n_lat_tiles, i % tps, tps)

    row = lambda w: pl.BlockSpec((tm, w), lambda i: (i, 0))
    mod = pl.BlockSpec((None, 1, d), lambda i: (cond(i), 0, 0))
    tab = pl.BlockSpec((tm, LANES), lambda i: (pos(i), 0))
    widths = (1024, 512, 256, 256, 512, 256, 256)
    return pl.pallas_call(
        _in_kernel,
        grid=(n_tiles,),
        in_specs=[row(d), mod, mod,
                  pl.BlockSpec((d, IN_COLS), lambda i: (0, 0)),
                  tab, tab, tab, tab,
                  pl.BlockSpec((1, LANES), lambda i: (0, 0))],
        out_specs=[row(w) for w in widths],
        out_shape=[jax.ShapeDtypeStruct((t_rows, w), BF16) for w in widths],
        compiler_params=_cparams(("parallel",)),
        name="in_proj",
    )(x, sh, sc, w_all, tabs["cs"], tabs["sn"], tabs["ck"], tabs["sk"], kvg)


CONV_ROWS = 32
PAD_ROWS = 16


def _local_kernel(a_ref, p_ref, cw_ref, cb_ref, lg_ref, lb_ref, pw_ref, ps_ref, ya_ref, yc_ref, vpad, upad,
                  *, ls):
    ch = min(256, ls)
    zeros = jnp.zeros((PAD_ROWS, CONV_CH), F32)
    vpad[0:PAD_ROWS, :] = zeros
    vpad[ls + PAD_ROWS:ls + 2 * PAD_ROWS, :] = zeros
    upad[0:PAD_ROWS, :] = zeros
    upad[ls + PAD_ROWS:ls + 2 * PAD_ROWS, :] = zeros

    def fill(c, carry):
        r0 = pl.multiple_of(c * ch, ch)
        a = a_ref[pl.ds(r0, ch), :].astype(F32)
        vpad[pl.ds(r0 + PAD_ROWS, ch), :] = a[:, :CONV_CH] * _sigmoid(a[:, CONV_CH:])
        upad[pl.ds(r0 + PAD_ROWS, ch), :] = p_ref[pl.ds(r0, ch), :].astype(F32)
        return carry

    lax.fori_loop(0, ls // ch, fill, 0)

    cw = cw_ref[...]
    cb, lg, lb = cb_ref[...], lg_ref[...], lb_ref[...]

    def conv(c, carry):
        r0 = pl.multiple_of(c * CONV_ROWS, CONV_ROWS)
        win = vpad[pl.ds(r0, CONV_ROWS + 2 * PAD_ROWS), :]
        acc = jnp.zeros((CONV_ROWS, CONV_CH), F32)
        for r in range(SUBLANES):
            wr = win[r:r + CONV_ROWS + 24, :]
            for a in range(4):
                k = SUBLANES * a + r - 1
                if 0 <= k < CONV_K:
                    acc = acc + wr[SUBLANES * a:SUBLANES * a + CONV_ROWS, :] * cw[k:k + 1, :]
        y = _ln(acc + cb) * lg + lb
        ya_ref[pl.ds(r0, CONV_ROWS), :] = (y * _sigmoid(y)).astype(BF16)
        return carry

    lax.fori_loop(0, ls // CONV_ROWS, conv, 0)

    ps = ps_ref[...]

    def pool(c, carry):
        r0 = pl.multiple_of(c * ch, ch)
        t = r0 + lax.broadcasted_iota(jnp.int32, (ch, LANES), 0)
        for g, w in enumerate(POOL_WINDOWS):
            lanes = slice(LANES * g, LANES * (g + 1))
            uwin = upad[pl.ds(r0, ch + 2 * PAD_ROWS), lanes]
            s = None
            for o in range(PAD_ROWS - w // 2, PAD_ROWS + w // 2):
                piece = uwin[o:o + ch, :]
                s = piece if s is None else s + piece
            lo = jnp.maximum(t - w // 2, 0)
            hi = jnp.minimum(t + w // 2, ls)
            d = s / (hi - lo).astype(F32) - uwin[PAD_ROWS:PAD_ROWS + ch, :]
            y = jnp.dot(d.astype(BF16), pw_ref[g], preferred_element_type=F32) * ps[:, lanes]
            yc_ref[pl.ds(r0, ch), lanes] = y.astype(BF16)
        return carry

    lax.fori_loop(0, ls // ch, pool, 0)


def _local_mixers(a, p, lw, ls, blk0, nseq):
    seq_in = lambda w: pl.BlockSpec((ls, w), lambda b: (blk0 + b, 0), pipeline_mode=pl.Buffered(1))
    seq_out = pl.BlockSpec((ls, 512), lambda b: (b, 0))
    full = lambda shape: pl.BlockSpec(shape, lambda b: tuple(0 for _ in shape))
    return pl.pallas_call(
        functools.partial(_local_kernel, ls=ls),
        grid=(nseq,),
        in_specs=[seq_in(1024), seq_in(512), full((32, CONV_CH)), full((1, CONV_CH)), full((1, CONV_CH)),
                  full((1, CONV_CH)), full((4, POOL_GROUP, POOL_GROUP)), full((1, POOL_CH))],
        out_specs=[seq_out, seq_out],
        out_shape=[jax.ShapeDtypeStruct((nseq * ls, 512), BF16)] * 2,
        scratch_shapes=[pltpu.VMEM((ls + 2 * PAD_ROWS, CONV_CH), F32),
                        pltpu.VMEM((ls + 2 * PAD_ROWS, POOL_CH), F32)],
        compiler_params=_cparams(("parallel",)),
        name="local_mixers",
    )(a, p, lw["conv_w"], lw["conv_b"], lw["conv_ln_g"], lw["conv_ln_b"], lw["pool_w"], lw["pool_scale"])


def _swa_heads(q, k2, v2, ok, sink_ref, o_ref):
    bq = q.shape[0]
    lane = lax.broadcasted_iota(jnp.int32, (bq, LANES), 1)
    low = lane < 64
    zero = jnp.zeros((), BF16)
    dn = (((1,), (1,)), ((), ()))
    for j in range(SWA_HEADS // 2):
        g = j // 2
        lo_sl = slice(0, LANES) if g == 0 else slice(LANES, 2 * LANES)
        hi_sl = slice(LANES, 2 * LANES) if g == 0 else slice(0, LANES)
        q2 = q[:, LANES * j:LANES * (j + 1)]
        outs = []
        for e in range(2):
            sl = lo_sl if e == 0 else hi_sl
            qm = jnp.where(low if e == 0 else ~low, q2, zero)
            s = lax.dot_general(qm, k2[:, sl], dn, preferred_element_type=F32)
            if ok is not None:
                s = jnp.where(ok, s, MASK_VALUE)
            sk = sink_ref[2 * j + e:2 * j + e + 1, 0:1]
            m = jnp.maximum(jnp.max(s, axis=-1, keepdims=True), sk)
            pexp = jnp.exp(s - m)
            den = jnp.sum(pexp, axis=-1, keepdims=True) + jnp.exp(sk - m)
            o = jnp.dot(pexp.astype(BF16), v2[:, sl], preferred_element_type=F32)
            outs.append(o / den)
        o_ref[:, LANES * j:LANES * (j + 1)] = jnp.where(low, outs[0], outs[1]).astype(BF16)


def _swa_lat_kernel(q_ref, kp_ref, kc_ref, kn_ref, vp_ref, vc_ref, vn_ref, kx_ref, vx_ref, sink_ref,
                    o_ref, *, seq_len):
    i = pl.program_id(1)
    k2 = jnp.concatenate([kp_ref[...], kc_ref[...], kn_ref[...], kx_ref[...]], axis=0)
    v2 = jnp.concatenate([vp_ref[...], vc_ref[...], vn_ref[...], vx_ref[...]], axis=0)
    nk = k2.shape[0]
    r = lax.broadcasted_iota(jnp.int32, (SWA_BLOCK, nk), 0)
    s = lax.broadcasted_iota(jnp.int32, (SWA_BLOCK, nk), 1)
    kpos = (i - 1) * SWA_BLOCK + s
    diff = kpos - (i * SWA_BLOCK + r)
    ok = ((kpos >= 0) & (kpos < seq_len) & (jnp.abs(diff) <= SWA_WINDOW)) | (s >= 3 * SWA_BLOCK)
    _swa_heads(q_ref[...], k2, v2, ok, sink_ref, o_ref)


def _swa_ctx_kernel(q_ref, kx_ref, vx_ref, sink_ref, o_ref):
    _swa_heads(q_ref[...], kx_ref[...], vx_ref[...], None, sink_ref, o_ref)


def _swa(q, k2, v2, sink_b, geo, with_ctx):
    nb, ls, lc = geo["B"], geo["L"], geo["C"]
    nq = ls // SWA_BLOCK
    ctx_blk0 = nb * ls // lc
    qspec = pl.BlockSpec((SWA_BLOCK, 512), lambda b, i: (b * nq + i, 0))
    kv = lambda off: pl.BlockSpec(
        (SWA_BLOCK, 256), lambda b, i: (b * nq + jnp.clip(i + off, 0, nq - 1), 0))
    cx = pl.BlockSpec((lc, 256), lambda b, i: (ctx_blk0 + b, 0))
    sk = pl.BlockSpec((SUBLANES, LANES), lambda b, i: (0, 0))
    yb = pl.pallas_call(
        functools.partial(_swa_lat_kernel, seq_len=ls),
        grid=(nb, nq),
        in_specs=[qspec, kv(-1), kv(0), kv(1), kv(-1), kv(0), kv(1), cx, cx, sk],
        out_specs=qspec,
        out_shape=jax.ShapeDtypeStruct((nb * ls, 512), BF16),
        compiler_params=_cparams(("parallel", "parallel")),
        name="swa_latent",
    )(q, k2, k2, k2, v2, v2, v2, k2, v2, sink_b)
    if not with_ctx:
        return yb, None
    cx1 = pl.BlockSpec((lc, 256), lambda b: (ctx_blk0 + b, 0))
    yb_ctx = pl.pallas_call(
        _swa_ctx_kernel,
        grid=(nb,),
        in_specs=[pl.BlockSpec((lc, 512), lambda b: (ctx_blk0 + b, 0)), cx1, cx1,
                  pl.BlockSpec((SUBLANES, LANES), lambda b: (0, 0))],
        out_specs=pl.BlockSpec((lc, 512), lambda b: (b, 0)),
        out_shape=jax.ShapeDtypeStruct((nb * lc, 512), BF16),
        compiler_params=_cparams(("parallel",)),
        name="swa_context",
    )(q, k2, v2, sink_b)
    return yb, yb_ctx


MLA_KC = 256
MLA_ROWS = MLA_HEADS * MLA_BLOCK
MLA_SOFTMAX_ROWS = 64
LOG2E = 1.4426950408889634


def _mla_kernel(cq_ref, *rest, n_lat, n_ctx):
    if n_lat:
        kl_ref, rest = rest[0], rest[1:]
    (kx_ref, qg_ref, wuq_ref, wukp_ref, wuvp_ref, cs_ref, sn_ref, o_ref,
     q_sc, s_a, s_b, p_a, p_b, al_a, al_b, m_sc, acc_sc) = rest
    kc = MLA_KC
    n = n_lat + n_ctx
    scale = (MLA_NOPE + MLA_ROPE) ** -0.5 * LOG2E
    cq = cq_ref[...].astype(F32)
    cqn = cq * lax.rsqrt(jnp.mean(cq * cq, axis=-1, keepdims=True) + LN_EPS) * qg_ref[...]
    q = jnp.dot(cqn.astype(BF16), wuq_ref[...], preferred_element_type=F32)
    qr = q[:, 512:768]
    lane2 = lax.broadcasted_iota(jnp.int32, (MLA_BLOCK, 2 * LANES), 1)
    rot = jnp.where((lane2 % 32) < 16, pltpu.roll(qr, 2 * LANES - 16, 1), pltpu.roll(qr, 16, 1))
    qr = (qr * cs_ref[...] + rot * sn_ref[...]) * scale
    slot_of_lane = lax.broadcasted_iota(jnp.int32, (MLA_BLOCK, LANES), 1) // 32
    for j in range(MLA_HEADS // 2):
        qa2 = jnp.dot(q[:, LANES * j:LANES * (j + 1)].astype(BF16), wukp_ref[j],
                      preferred_element_type=F32) * scale
        for e in range(2):
            h = 2 * j + e
            rows = slice(h * MLA_BLOCK, (h + 1) * MLA_BLOCK)
            qrh = jnp.where(slot_of_lane == (h % 4), qr[:, LANES * (h // 4):LANES * (h // 4 + 1)], 0.0)
            q_sc[rows, 0:LANES] = qa2[:, LANES * e:LANES * (e + 1)].astype(BF16)
            q_sc[rows, LANES:2 * LANES] = qrh.astype(BF16)

    m_sc[...] = jnp.full(m_sc.shape, MASK_VALUE, F32)
    acc_sc[...] = jnp.zeros(acc_sc.shape, F32)
    dn = (((1,), (1,)), ((), ()))
    s_bufs, p_bufs, al_bufs = (s_a, s_b), (p_a, p_b), (al_a, al_b)
    ones = jnp.ones((kc, LANES), BF16)

    def keys(c):
        if isinstance(c, int):
            if c >= n_lat:
                return kx_ref[(c - n_lat) * kc:(c - n_lat + 1) * kc, :]
            return kl_ref[c * kc:(c + 1) * kc, :]
        return kl_ref[pl.ds(pl.multiple_of(c * kc, kc), kc), :]

    def scores(c, par):
        s_bufs[par][...] = lax.dot_general(q_sc[...], keys(c), dn, preferred_element_type=F32)

    def softmax(par):
        s_ref, p_ref, al_ref = s_bufs[par], p_bufs[par], al_bufs[par]
        for rb in range(MLA_ROWS // MLA_SOFTMAX_ROWS):
            rows = slice(rb * MLA_SOFTMAX_ROWS, (rb + 1) * MLA_SOFTMAX_ROWS)
            sv = s_ref[rows, :]
            m_old = m_sc[rows, :]
            m_new = jnp.maximum(m_old, jnp.max(sv, axis=-1, keepdims=True))
            al_ref[rows, :] = jnp.exp2(m_old - m_new)
            m_sc[rows, :] = m_new
            for t in range(kc // LANES):
                lanes = slice(LANES * t, LANES * (t + 1))
                p_ref[rows, lanes] = jnp.exp2(sv[:, lanes] - m_new).astype(BF16)

    def values(c, par):
        v_aug = jnp.concatenate([keys(c)[:, 0:LANES], ones], axis=1)
        pv = jnp.dot(p_bufs[par][...], v_aug, preferred_element_type=F32)
        al = al_bufs[par][...]
        acc_sc[...] = jnp.concatenate([al, al], axis=1) * acc_sc[...] + pv

    def step(t, par):
        static = isinstance(t, int)
        if not static or t < n:
            scores(t, par)
        if not static or 1 <= t <= n:
            softmax(1 - par)
        if not static or 2 <= t <= n + 1:
            values(t - 2, par)

    loop_lo = 2
    loop_hi = max(loop_lo, n_lat)
    n_pairs = (loop_hi - loop_lo) // 2
    for t in range(0, loop_lo):
        step(t, t % 2)
    if n_pairs:
        def pair(jp, carry):
            t0 = loop_lo + 2 * jp
            step(t0, 0)
            step(t0 + 1, 1)
            return carry
        lax.fori_loop(0, n_pairs, pair, 0)
    for t in range(loop_lo + 2 * n_pairs, n + 2):
        step(t, t % 2)

    acc = acc_sc[...]
    o = acc[:, 0:LANES] / acc[:, LANES:2 * LANES]
    for j in range(MLA_HEADS // 2):
        o2 = jnp.concatenate([o[(2 * j) * MLA_BLOCK:(2 * j + 1) * MLA_BLOCK, :],
                              o[(2 * j + 1) * MLA_BLOCK:(2 * j + 2) * MLA_BLOCK, :]], axis=1)
        o_ref[:, LANES * j:LANES * (j + 1)] = jnp.dot(
            o2.astype(BF16), wuvp_ref[j], preferred_element_type=F32).astype(BF16)


def _mla(cq, kc_all, lw, tabs, geo, lat):
    nb, ls, lc = geo["B"], geo["L"], geo["C"]
    ctx_blk0 = nb * ls // lc
    if lat:
        nq = ls // MLA_BLOCK
        qmap = lambda b, i: (b * nq + i, 0)
        tmap = lambda b, i: (i, 0)
        n_lat = ls // MLA_KC
    else:
        nq = lc // MLA_BLOCK
        qmap = lambda b, i: (nb * (ls // MLA_BLOCK) + b * nq + i, 0)
        tmap = lambda b, i: (ls // MLA_BLOCK, 0)
        n_lat = 0
    omap = lambda b, i: (b * nq + i, 0)
    full = lambda shape: pl.BlockSpec(shape, lambda b, i: tuple(0 for _ in shape))
    in_specs = [pl.BlockSpec((MLA_BLOCK, 256), qmap)]
    args = [cq]
    if lat:
        in_specs.append(pl.BlockSpec((ls, 256), lambda b, i: (b, 0)))
        args.append(kc_all)
    in_specs += [pl.BlockSpec((lc, 256), lambda b, i: (ctx_blk0 + b, 0)),
                 full((1, 256)), full((256, 768)), full((4, LANES, 256)), full((4, 256, LANES)),
                 pl.BlockSpec((MLA_BLOCK, 256), tmap), pl.BlockSpec((MLA_BLOCK, 256), tmap)]
    args += [kc_all, lw["mla_q_g"], lw["wuq"], lw["wukp"], lw["wuvp"], tabs["cq"], tabs["sq"]]
    return pl.pallas_call(
        functools.partial(_mla_kernel, n_lat=n_lat, n_ctx=lc // MLA_KC),
        grid=(nb, nq),
        in_specs=in_specs,
        out_specs=pl.BlockSpec((MLA_BLOCK, 512), omap),
        out_shape=jax.ShapeDtypeStruct((nb * nq * MLA_BLOCK, 512), BF16),
        scratch_shapes=[pltpu.VMEM((MLA_ROWS, 256), BF16),
                        pltpu.VMEM((MLA_ROWS, MLA_KC), F32), pltpu.VMEM((MLA_ROWS, MLA_KC), F32),
                        pltpu.VMEM((MLA_ROWS, MLA_KC), BF16), pltpu.VMEM((MLA_ROWS, MLA_KC), BF16),
                        pltpu.VMEM((MLA_ROWS, LANES), F32), pltpu.VMEM((MLA_ROWS, LANES), F32),
                        pltpu.VMEM((MLA_ROWS, LANES), F32), pltpu.VMEM((MLA_ROWS, 2 * LANES), F32)],
        compiler_params=_cparams(("parallel", "parallel")),
        name="mla_latent" if lat else "mla_context",
    )(*args)


def _merge_kernel(x_ref, *rest, alpha, n_lat_tiles, has_ctx):
    n_y = 2 * N_BRANCH if has_ctx else N_BRANCH
    y_refs, rest = rest[:n_y], rest[n_y:]
    (sh1_ref, sc1_ref, g1_ref, sh2_ref, sc2_ref, wg_ref, bg_ref, wb_ref, wo_ref, l1g_ref, l1b_ref,
     rw_ref, rb_ref, x1_ref, h2_ref, ei_ref, ew_ref) = rest
    tm = x_ref.shape[0]
    is_ctx = pl.program_id(0) >= n_lat_tiles
    x = x_ref[...]
    h = (_ln(x) * (1.0 + sc1_ref[...]) + sh1_ref[...]).astype(BF16)
    merged = None
    for i in range(N_BRANCH):
        yi = y_refs[i][...]
        if has_ctx:
            yi = jnp.where(is_ctx, y_refs[N_BRANCH + i][...], yi)
        gate = _sigmoid(jnp.dot(h, wg_ref[i], preferred_element_type=F32) + bg_ref[i])
        term = gate * jnp.dot(yi, wb_ref[i], preferred_element_type=F32)
        merged = term if merged is None else merged + term
    y = jnp.dot(merged.astype(BF16), wo_ref[...], preferred_element_type=F32)
    x1 = _ln(alpha * x + g1_ref[...] * y) * l1g_ref[...] + l1b_ref[...]
    x1_ref[...] = x1
    h2 = _ln(x1) * (1.0 + sc2_ref[...]) + sh2_ref[...]
    for j in range(ROW_TILES):
        h2_ref[pl.ds(j, tm, stride=ROW_TILES), :] = h2[:, LANES * j:LANES * (j + 1)]
    lg = jnp.dot(h2, rw_ref[...], precision=HIGHEST, preferred_element_type=F32) + rb_ref[...]
    lane = lax.broadcasted_iota(jnp.int32, (tm, LANES), 1).astype(F32)
    vals, idxs = [], []
    for _ in range(TOP_K):
        mx = jnp.max(lg, axis=-1, keepdims=True)
        ix = jnp.min(jnp.where(lg == mx, lane, float(LANES)), axis=-1, keepdims=True)
        vals.append(mx)
        idxs.append(ix)
        lg = jnp.where(lane == ix, -3.0e38, lg)
    exps = [jnp.exp(v - vals[0]) for v in vals]
    den = exps[0] + exps[1] + exps[2] + exps[3]
    ei = jnp.zeros((tm, LANES), F32)
    ew = jnp.zeros((tm, LANES), F32)
    for k in range(TOP_K):
        ei = jnp.where(lane == float(k), idxs[k], ei)
        ew = jnp.where(lane == float(k), exps[k] / den, ew)
    ei_ref[...] = ei.astype(jnp.int32)
    ew_ref[...] = ew


def _merge(x, ys_lat, ys_ctx, mods, lw, geo, alpha):
    tm, n_lat_tiles, tps, nb = geo["tm"], geo["n_lat_tiles"], geo["tps"], geo["B"]
    d = x.shape[1]
    has_ctx = ys_ctx is not None
    n_tiles = geo["n_tiles"] if has_ctx else n_lat_tiles
    n_ctx_tiles = geo["n_tiles"] - n_lat_tiles
    t_rows = n_tiles * tm

    def cond(i):
        return jnp.where(i < n_lat_tiles, i // tps, nb)

    row = lambda w: pl.BlockSpec((tm, w), lambda i: (i, 0))
    y_lat = pl.BlockSpec((tm, BRANCH_W), lambda i: (jnp.minimum(i, n_lat_tiles - 1), 0))
    y_ctx = pl.BlockSpec((tm, BRANCH_W), lambda i: (jnp.clip(i - n_lat_tiles, 0, n_ctx_tiles - 1), 0))
    y_specs = [y_lat] * N_BRANCH + ([y_ctx] * N_BRANCH if has_ctx else [])
    ys = tuple(ys_lat) + (tuple(ys_ctx) if has_ctx else ())
    mod = pl.BlockSpec((None, 1, d), lambda i: (cond(i), 0, 0))
    full = lambda shape: pl.BlockSpec(shape, lambda i: tuple(0 for _ in shape), pipeline_mode=pl.Buffered(1))
    return pl.pallas_call(
        functools.partial(_merge_kernel, alpha=alpha, n_lat_tiles=n_lat_tiles, has_ctx=has_ctx),
        grid=(n_tiles,),
        name="merge_router",
        in_specs=[row(d)] + y_specs + [mod, mod, mod, mod, mod,
                  full((4, d, d)), full((4, 1, d)), full((4, BRANCH_W, d)), full((d, d)),
                  full((1, d)), full((1, d)), full((d, LANES)), full((1, LANES))],
        out_specs=[row(d), pl.BlockSpec((tm * ROW_TILES, LANES), lambda i: (i, 0)), row(LANES), row(LANES)],
        out_shape=[jax.ShapeDtypeStruct((t_rows, d), F32),
                   jax.ShapeDtypeStruct((t_rows * ROW_TILES, LANES), F32),
                   jax.ShapeDtypeStruct((t_rows, LANES), jnp.int32),
                   jax.ShapeDtypeStruct((t_rows, LANES), F32)],
        compiler_params=_cparams(("parallel",)),
    )(x, *ys, mods["sh1"], mods["sc1"], mods["g1"], mods["sh2"], mods["sc2"],
      lw["w_gate"], lw["b_gate"], lw["w_branch"], lw["w_out"], lw["ln1_g"], lw["ln1_b"],
      lw["router_w"], lw["router_b"])


GATHER_UNROLL = 8


def _gmm_kernel(blk_e_ref, nvalid_ref, idx_ref, h2_hbm, wgu_ref, bgu_ref, wdn_ref, bdn_ref,
                ytk_hbm, idx_sm, xbuf, ybuf, sems):
    del blk_e_ref
    b = pl.program_id(0)
    bm = xbuf.shape[0] // ROW_TILES

    @pl.when(b == 0)
    def _():
        ybuf[...] = jnp.zeros(ybuf.shape, F32)
        pad0 = ytk_hbm.shape[0] - bm * ROW_TILES
        cp = pltpu.make_async_copy(ybuf, ytk_hbm.at[pl.ds(pad0, bm * ROW_TILES), :], sems.at[2])
        cp.start()
        cp.wait()

    @pl.when(b < nvalid_ref[0])
    def _():
        cp = pltpu.make_async_copy(idx_ref.at[0], idx_sm, sems.at[0])
        cp.start()
        cp.wait()

        def gather(o, carry):
            for u in range(GATHER_UNROLL):
                r = o * GATHER_UNROLL + u
                src = pl.multiple_of(idx_sm[0, r] * ROW_TILES, ROW_TILES)
                dst = pl.multiple_of(r * ROW_TILES, ROW_TILES)
                pltpu.make_async_copy(h2_hbm.at[pl.ds(src, ROW_TILES), :],
                                      xbuf.at[pl.ds(dst, ROW_TILES), :], sems.at[1]).start()
            return carry

        lax.fori_loop(0, bm // GATHER_UNROLL, gather, 0)
        pltpu.make_async_copy(h2_hbm.at[pl.ds(0, bm * ROW_TILES), :], xbuf, sems.at[1]).wait()

        xs = jnp.concatenate(
            [xbuf[pl.ds(j, bm, stride=ROW_TILES), :] for j in range(ROW_TILES)], axis=1).astype(BF16)
        z = jnp.dot(xs, wgu_ref[...], preferred_element_type=F32) + bgu_ref[...]
        gate = jnp.minimum(z[:, :D_EXPERT], SWIGLU_LIMIT)
        up = jnp.clip(z[:, D_EXPERT:], -SWIGLU_LIMIT, SWIGLU_LIMIT)
        act = (up + 1.0) * gate * _sigmoid(SWIGLU_ALPHA * gate)
        y = jnp.dot(act.astype(BF16), wdn_ref[...], preferred_element_type=F32) + bdn_ref[...]
        for j in range(ROW_TILES):
            ybuf[pl.ds(j, bm, stride=ROW_TILES), :] = y[:, LANES * j:LANES * (j + 1)]

        def scatter(o, carry):
            for u in range(GATHER_UNROLL):
                r = o * GATHER_UNROLL + u
                src = pl.multiple_of(r * ROW_TILES, ROW_TILES)
                dst = pl.multiple_of(idx_sm[1, r] * ROW_TILES, ROW_TILES)
                pltpu.make_async_copy(ybuf.at[pl.ds(src, ROW_TILES), :],
                                      ytk_hbm.at[pl.ds(dst, ROW_TILES), :], sems.at[2]).start()
            return carry

        lax.fori_loop(0, bm // GATHER_UNROLL, scatter, 0)
        pltpu.make_async_copy(ybuf, ytk_hbm.at[pl.ds(0, bm * ROW_TILES), :], sems.at[2]).wait()


def _gmm(h2g, plan, lw, n_tok):
    bm = EXPERT_BM
    n_blocks = plan["idx"].shape[0]
    d = lw["w_gu"].shape[1]
    n_slots = n_tok * TOP_K + bm
    grid_spec = pltpu.PrefetchScalarGridSpec(
        num_scalar_prefetch=2,
        grid=(n_blocks,),
        in_specs=[
            pl.BlockSpec((1, 2, bm), lambda b, be, nv: (b, 0, 0)),
            pl.BlockSpec(memory_space=pl.ANY),
            pl.BlockSpec((None, d, 2 * D_EXPERT), lambda b, be, nv: (be[b], 0, 0)),
            pl.BlockSpec((None, 1, 2 * D_EXPERT), lambda b, be, nv: (be[b], 0, 0)),
            pl.BlockSpec((None, D_EXPERT, d), lambda b, be, nv: (be[b], 0, 0)),
            pl.BlockSpec((None, 1, d), lambda b, be, nv: (be[b], 0, 0)),
        ],
        out_specs=pl.BlockSpec(memory_space=pl.ANY),
        scratch_shapes=[pltpu.SMEM((2, bm), jnp.int32),
                        pltpu.VMEM((bm * ROW_TILES, LANES), F32),
                        pltpu.VMEM((bm * ROW_TILES, LANES), F32),
                        pltpu.SemaphoreType.DMA((3,))],
    )
    return pl.pallas_call(
        _gmm_kernel,
        grid_spec=grid_spec,
        out_shape=jax.ShapeDtypeStruct((n_slots * ROW_TILES, LANES), F32),
        compiler_params=_cparams(("arbitrary",)),
        name="expert_gmm",
    )(plan["blk_e"], plan["nvalid"], plan["idx"], h2g, lw["w_gu"], lw["b_gu"], lw["w_down"], lw["b_down"])


def _plan(ei, n_tok):
    bm = EXPERT_BM
    n_assign = n_tok * TOP_K
    flat_e = ei.reshape(-1)
    order = jnp.argsort(flat_e, stable=True).astype(jnp.int32)
    sorted_e = flat_e[order]
    counts = jnp.sum((flat_e[:, None] == jnp.arange(N_EXPERTS, dtype=jnp.int32)[None, :]).astype(jnp.int32), axis=0)
    padded = (counts + bm - 1) // bm * bm
    pad_end = jnp.cumsum(padded)
    pad_start = pad_end - padded
    grp_start = jnp.cumsum(counts) - counts
    dest = pad_start[sorted_e] + jnp.arange(n_assign, dtype=jnp.int32) - grp_start[sorted_e]
    n_blocks = -(-(n_assign + N_EXPERTS * (bm - 1)) // bm)
    n_rows = n_blocks * bm
    row_asg = jnp.full((n_rows,), -1, jnp.int32).at[dest].set(order)
    valid = row_asg >= 0
    row_src = jnp.where(valid, row_asg // TOP_K, 0)
    row_dst = jnp.where(valid, row_asg, n_assign + jnp.arange(n_rows, dtype=jnp.int32) % bm)
    idx = jnp.stack([row_src.reshape(n_blocks, bm), row_dst.reshape(n_blocks, bm)], axis=1)
    blk_e = jnp.minimum(jnp.searchsorted(pad_end, jnp.arange(n_blocks, dtype=jnp.int32) * bm, side="right"),
                        N_EXPERTS - 1).astype(jnp.int32)
    nvalid = (pad_end[-1:] // bm).astype(jnp.int32)
    return {"idx": idx.astype(jnp.int32), "blk_e": blk_e, "nvalid": nvalid}


def _combine_kernel(x_ref, ytk_ref, ew_ref, g2_ref, lg_ref, lb_ref, o_ref, *, alpha):
    tm = x_ref.shape[0]
    ew = ew_ref[...]
    stride = TOP_K * ROW_TILES
    cols = []
    for j in range(ROW_TILES):
        acc = None
        for k in range(TOP_K):
            piece = ytk_ref[pl.ds(k * ROW_TILES + j, tm, stride=stride), :] * ew[:, k:k + 1]
            acc = piece if acc is None else acc + piece
        cols.append(acc)
    m = jnp.concatenate(cols, axis=1)
    o_ref[...] = _ln(alpha * x_ref[...] + g2_ref[...] * m) * lg_ref[...] + lb_ref[...]


def _combine(x1, ytk, ew, g2, lw, geo, n_tiles, alpha):
    tm, n_lat_tiles, tps, nb = geo["tmc"], geo["n_lat_tiles"], geo["tps"], geo["B"]
    ratio = geo["tm"] // tm
    d = x1.shape[1]

    def cond(i):
        return jnp.where(i < n_lat_tiles * ratio, i // (tps * ratio), nb)

    row = lambda w: pl.BlockSpec((tm, w), lambda i: (i, 0))
    full = lambda shape: pl.BlockSpec(shape, lambda i: tuple(0 for _ in shape))
    return pl.pallas_call(
        functools.partial(_combine_kernel, alpha=alpha),
        grid=(n_tiles * ratio,),
        in_specs=[row(d), pl.BlockSpec((tm * TOP_K * ROW_TILES, LANES), lambda i: (i, 0)), row(LANES),
                  pl.BlockSpec((None, 1, d), lambda i: (cond(i), 0, 0)), full((1, d)), full((1, d))],
        out_specs=row(d),
        out_shape=jax.ShapeDtypeStruct((n_tiles * tm * ratio, d), F32),
        compiler_params=_cparams(("parallel",)),
        name="moe_combine",
    )(x1, ytk, ew, g2, lw["ln2_g"], lw["ln2_b"])


def _rope_tables(seq_len, tm):
    rows = seq_len // GRID_W
    row = jnp.repeat(jnp.arange(rows, dtype=F32), GRID_W)
    col = jnp.tile(jnp.arange(GRID_W, dtype=F32), rows)

    def table(rot_dim, reps, pad_rows):
        n_freq = rot_dim // 4
        inv = ROPE_THETA ** (-jnp.arange(n_freq, dtype=F32) / n_freq)
        ang = jnp.concatenate([row[:, None] * inv, col[:, None] * inv], axis=-1)
        cos, sin = jnp.cos(ang), jnp.sin(ang)
        c = jnp.tile(jnp.concatenate([cos, cos], axis=-1), (1, reps))
        s = jnp.tile(jnp.concatenate([-sin, sin], axis=-1), (1, reps))
        c = jnp.concatenate([c, jnp.ones((pad_rows, c.shape[1]), F32)], axis=0)
        s = jnp.concatenate([s, jnp.zeros((pad_rows, s.shape[1]), F32)], axis=0)
        return c, s

    cs, sn = table(HEAD_DIM, 2, tm)
    ck, sk = table(MLA_ROPE, 4, tm)
    cq, sq = table(MLA_ROPE, 8, MLA_BLOCK)
    return {"cs": cs, "sn": sn, "ck": ck, "sk": sk, "cq": cq, "sq": sq}


def _layer_weights(l, w_in, conv_w, conv_b, conv_ln_g, conv_ln_b, swa_sink, pool_w, pool_scale, mla_q_g,
                   mla_w_uq, mla_kv_g, mla_w_uk, mla_w_uv, w_branch, w_gate, b_gate, w_out, ln1_g, ln1_b,
                   router_w, router_b, w_gu_bf, b_gu, w_down_bf, b_down, ln2_g, ln2_b):
    d = w_in.shape[1]
    wi = w_in[l]
    kr = wi[:, 2688:2720]
    w_all = jnp.concatenate([wi[:, :2688], kr, kr, kr, kr], axis=1).astype(BF16)
    wuq = mla_w_uq[l].reshape(MLA_Q_RANK, MLA_HEADS, MLA_NOPE + MLA_ROPE)
    wuq = jnp.concatenate([wuq[:, :, :MLA_NOPE].reshape(MLA_Q_RANK, -1),
                           wuq[:, :, MLA_NOPE:].reshape(MLA_Q_RANK, -1)], axis=1).astype(BF16)
    wuk = jnp.transpose(mla_w_uk[l], (1, 2, 0))
    wuv = jnp.transpose(mla_w_uv[l], (1, 0, 2))
    zk = jnp.zeros((MLA_NOPE, MLA_KV_RANK), F32)
    zv = jnp.zeros((MLA_KV_RANK, MLA_V), F32)
    wukp = jnp.stack([jnp.block([[wuk[2 * j], zk], [zk, wuk[2 * j + 1]]]) for j in range(MLA_HEADS // 2)])
    wuvp = jnp.stack([jnp.block([[wuv[2 * j], zv], [zv, wuv[2 * j + 1]]]) for j in range(MLA_HEADS // 2)])
    rw = jnp.zeros((d, LANES), F32).at[:, :N_EXPERTS].set(router_w[l])
    rb = jnp.full((1, LANES), MASK_VALUE, F32).at[0, :N_EXPERTS].set(router_b[l])
    return {
        "w_all": w_all,
        "conv_w": jnp.concatenate([conv_w[l], jnp.zeros((1, CONV_CH), F32)], axis=0),
        "conv_b": conv_b[l][None], "conv_ln_g": conv_ln_g[l][None], "conv_ln_b": conv_ln_b[l][None],
        "sink": jnp.broadcast_to(swa_sink[l][:, None], (SWA_HEADS, LANES)),
        "pool_w": pool_w[l].astype(BF16), "pool_scale": pool_scale[l][None],
        "mla_q_g": mla_q_g[l][None], "mla_kv_g": mla_kv_g[l][None],
        "wuq": wuq, "wukp": wukp.astype(BF16), "wuvp": wuvp.astype(BF16),
        "w_branch": w_branch[l].astype(BF16), "w_gate": w_gate[l].astype(BF16),
        "b_gate": b_gate[l][:, None, :], "w_out": w_out[l].astype(BF16),
        "ln1_g": ln1_g[l][None], "ln1_b": ln1_b[l][None],
        "router_w": rw, "router_b": rb,
        "w_gu": w_gu_bf[l], "b_gu": b_gu[l][:, None, :],
        "w_down": w_down_bf[l], "b_down": b_down[l][:, None, :],
        "ln2_g": ln2_g[l][None], "ln2_b": ln2_b[l][None],
    }


def _token_tile(seq_len, ctx_rows):
    for tm in (512, 256, 128):
        if seq_len % tm == 0 and ctx_rows % tm == 0:
            return tm
    raise ValueError("unsupported sequence / context lengths")


def kernel(x, c, ctx, c_ctx, w_ada, b_ada, w_in, conv_w, conv_b, conv_ln_g, conv_ln_b, swa_sink, pool_w,
           pool_scale, mla_q_g, mla_w_uq, mla_kv_g, mla_w_uk, mla_w_uv, w_branch, w_gate, b_gate, w_out,
           ln1_g, ln1_b, router_w, router_b, w_gu, b_gu, w_down, b_down, ln2_g, ln2_b):
    nb, ls, d = x.shape
    lc = ctx.shape[1]
    depth = w_ada.shape[0]
    assert ls % GRID_W == 0 and ls % SWA_BLOCK == 0 and (nb * ls) % lc == 0
    assert ls % MLA_KC == 0 and lc % MLA_KC == 0 and lc % MLA_BLOCK == 0
    tm = _token_tile(ls, nb * lc)
    t_lat, t_ctx = nb * ls, nb * lc
    geo = {"B": nb, "L": ls, "C": lc, "tm": tm, "tmc": min(tm, 256), "tps": ls // tm,
           "n_lat_tiles": t_lat // tm, "n_tiles": (t_lat + t_ctx) // tm}
    alpha = (2 * depth) ** 0.25

    ra = -(-(nb + 1) // SUBLANES) * SUBLANES
    cvec = jnp.zeros((ra, d), F32).at[:nb].set(c).at[nb].set(c_ctx)
    ada = _ada_terms(cvec, w_ada, b_ada).reshape(depth, ra, 6, 1, d)
    tabs = _rope_tables(ls, tm)
    w_gu_bf = w_gu.astype(BF16)
    w_down_bf = w_down.astype(BF16)

    xa = jnp.concatenate([x.reshape(t_lat, d), ctx.reshape(t_ctx, d)], axis=0)
    for l in range(depth):
        ctx_out = l < depth - 1
        n_tiles = geo["n_tiles"] if ctx_out else geo["n_lat_tiles"]
        names = ("sh1", "sc1", "g1", "sh2", "sc2", "g2")
        mods = {n: ada[l, :, i] for i, n in enumerate(names)}
        lw = _layer_weights(l, w_in, conv_w, conv_b, conv_ln_g, conv_ln_b, swa_sink, pool_w, pool_scale,
                            mla_q_g, mla_w_uq, mla_kv_g, mla_w_uk, mla_w_uv, w_branch, w_gate, b_gate,
                            w_out, ln1_g, ln1_b, router_w, router_b, w_gu_bf, b_gu, w_down_bf, b_down,
                            ln2_g, ln2_b)
        a, q, k2, v2, p, cq, kc = _in_proj(xa, mods["sh1"], mods["sc1"], lw["w_all"], tabs,
                                           lw["mla_kv_g"], geo)
        ya, yc = _local_mixers(a, p, lw, ls, 0, nb)
        yb, yb_ctx = _swa(q, k2, v2, lw["sink"], geo, ctx_out)
        yd = _mla(cq, kc, lw, tabs, geo, True)
        ys_ctx = None
        if ctx_out:
            ya_ctx, yc_ctx = _local_mixers(a, p, lw, lc, t_lat // lc, nb)
            yd_ctx = _mla(cq, kc, lw, tabs, geo, False)
            ys_ctx = (ya_ctx, yb_ctx, yc_ctx, yd_ctx)
        x1, h2g, ei, ew = _merge(xa, (ya, yb, yc, yd), ys_ctx, mods, lw, geo, alpha)
        n_tok = n_tiles * tm
        plan = _plan(ei[:, :TOP_K], n_tok)
        ytk = _gmm(h2g, plan, lw, n_tok)
        xa = _combine(x1, ytk, ew, mods["g2"], lw, geo, n_tiles, alpha)
    return xa[:t_lat].reshape(nb, ls, d)
```

```python
import functools

import jax
import jax.numpy as jnp
import numpy as np
from jax import lax
from jax.experimental import pallas as pl
from jax.experimental.pallas import tpu as pltpu

F32 = jnp.float32
BF16 = jnp.bfloat16
HIGHEST = lax.Precision.HIGHEST

GRID_W = 64
HEAD_DIM = 64
ROPE_THETA = 10000.0
LN_EPS = 1e-6
MASK_VALUE = -1e30
CONV_CH = 512
CONV_K = 31
SWA_HEADS = 8
SWA_WINDOW = 128
SWA_BLOCK = 128
POOL_CH = 512
POOL_WINDOWS = (2, 4, 8, 16)
POOL_GROUP = 128
MLA_HEADS = 8
MLA_Q_RANK = 256
MLA_KV_RANK = 128
MLA_NOPE = 64
MLA_ROPE = 32
MLA_V = 64
MLA_BLOCK = 128
N_BRANCH = 4
BRANCH_W = 512
N_EXPERTS = 32
TOP_K = 4
D_EXPERT = 1024
SWIGLU_LIMIT = 7.0
SWIGLU_ALPHA = 1.702

LANES = 128
SUBLANES = 8
ROW_TILES = 8
EXPERT_BM = 512
VMEM_LIMIT = 56 * 1024 * 1024


def _cparams(sem):
    return pltpu.CompilerParams(dimension_semantics=sem, vmem_limit_bytes=VMEM_LIMIT)


def _ln(x):
    mu = jnp.mean(x, axis=-1, keepdims=True)
    xc = x - mu
    var = jnp.mean(xc * xc, axis=-1, keepdims=True)
    return xc * lax.rsqrt(var + LN_EPS)


def _sigmoid(x):
    return 1.0 / (1.0 + jnp.exp(-x))


def _ada_kernel(c_ref, w_ref, b_ref, o_ref):
    c = c_ref[...]
    s = c * _sigmoid(c)
    o_ref[...] = jnp.dot(s, w_ref[...], precision=HIGHEST, preferred_element_type=F32) + b_ref[...]


def _ada_terms(cvec, w_ada, b_ada):
    depth, d, n = w_ada.shape
    ra = cvec.shape[0]
    tn = 1536
    return pl.pallas_call(
        _ada_kernel,
        grid=(depth, n // tn),
        in_specs=[
            pl.BlockSpec((ra, d), lambda l, j: (0, 0)),
            pl.BlockSpec((None, d, tn), lambda l, j: (l, 0, j)),
            pl.BlockSpec((None, 1, tn), lambda l, j: (l, 0, j)),
        ],
        out_specs=pl.BlockSpec((None, ra, tn), lambda l, j: (l, 0, j)),
        out_shape=jax.ShapeDtypeStruct((depth, ra, n), F32),
        compiler_params=_cparams(("parallel", "parallel")),
        name="ada_terms",
    )(cvec, w_ada, b_ada.reshape(depth, 1, n))


IN_COLS = 1024 + 512 + 128 + 128 + 512 + 256 + 128 + 128


def _in_kernel(x_ref, sh_ref, sc_ref, w_ref, cs_ref, sn_ref, ck_ref, sk_ref, kvg_ref,
               a_ref, q_ref, k2_ref, v2_ref, p_ref, cq_ref, kc_ref):
    tm = x_ref.shape[0]
    h = _ln(x_ref[...]) * (1.0 + sc_ref[...]) + sh_ref[...]
    z = jnp.dot(h.astype(BF16), w_ref[...], preferred_element_type=F32)
    a_ref[...] = z[:, 0:1024].astype(BF16)
    cs, sn = cs_ref[...], sn_ref[...]
    lane = lax.broadcasted_iota(jnp.int32, (tm, LANES), 1)
    first32 = (lane % 64) < 32

    def rope64(t):
        rot = jnp.where(first32, pltpu.roll(t, LANES - 32, 1), pltpu.roll(t, 32, 1))
        return t * cs + rot * sn

    for j in range(4):
        qj = rope64(z[:, 1024 + LANES * j:1024 + LANES * (j + 1)]) * (HEAD_DIM ** -0.5)
        q_ref[:, LANES * j:LANES * (j + 1)] = qj.astype(BF16)
    k = rope64(z[:, 1536:1664])
    k2_ref[:, 0:LANES] = k.astype(BF16)
    k2_ref[:, LANES:2 * LANES] = pltpu.roll(k, 64, 1).astype(BF16)
    v = z[:, 1664:1792]
    v2_ref[:, 0:LANES] = v.astype(BF16)
    v2_ref[:, LANES:2 * LANES] = pltpu.roll(v, 64, 1).astype(BF16)
    p_ref[...] = z[:, 1792:2304].astype(BF16)
    cq_ref[...] = z[:, 2304:2560].astype(BF16)
    ckv = z[:, 2560:2688]
    ckvn = ckv * lax.rsqrt(jnp.mean(ckv * ckv, axis=-1, keepdims=True) + LN_EPS) * kvg_ref[...]
    kr4 = z[:, 2688:2816]
    first16 = (lane % 32) < 16
    rot = jnp.where(first16, pltpu.roll(kr4, LANES - 16, 1), pltpu.roll(kr4, 16, 1))
    kr4 = kr4 * ck_ref[...] + rot * sk_ref[...]
    kc_ref[:, 0:LANES] = ckvn.astype(BF16)
    kc_ref[:, LANES:2 * LANES] = kr4.astype(BF16)


def _in_proj(x, sh, sc, w_all, tabs, kvg, geo):
    tm, n_tiles, n_lat_tiles, tps, nb = geo["tm"], geo["n_tiles"], geo["n_lat_tiles"], geo["tps"], geo["B"]
    t_rows, d = n_tiles * tm, x.shape[1]

    def cond(i):
        return jnp.where(i < n_lat_tiles, i // tps, nb)

    def pos(i):
        return jnp.where(i < n_lat_tiles, i % tps, tps)

    row = lambda w: pl.BlockSpec((tm, w), lambda i: (i, 0))
    mod = pl.BlockSpec((None, 1, d), lambda i: (cond(i), 0, 0))
    tab = pl.BlockSpec((tm, LANES), lambda i: (pos(i), 0))
    widths = (1024, 512, 256, 256, 512, 256, 256)
    return pl.pallas_call(
        _in_kernel,
        grid=(n_tiles,),
        in_specs=[row(d), mod, mod,
                  pl.BlockSpec((d, IN_COLS), lambda i: (0, 0)),
                  tab, tab, tab, tab,
                  pl.BlockSpec((1, LANES), lambda i: (0, 0))],
        out_specs=[row(w) for w in widths],
        out_shape=[jax.ShapeDtypeStruct((t_rows, w), BF16) for w in widths],
        compiler_params=_cparams(("parallel",)),
        name="in_proj",
    )(x, sh, sc, w_all, tabs["cs"], tabs["sn"], tabs["ck"], tabs["sk"], kvg)


CONV_ROWS = 32
PAD_ROWS = 16


def _local_kernel(a_ref, p_ref, cw_ref, cb_ref, lg_ref, lb_ref, pw_ref, ps_ref, ya_ref, yc_ref, vpad, upad,
                  *, ls):
    ch = min(256, ls)
    zeros = jnp.zeros((PAD_ROWS, CONV_CH), F32)
    vpad[0:PAD_ROWS, :] = zeros
    vpad[ls + PAD_ROWS:ls + 2 * PAD_ROWS, :] = zeros
    upad[0:PAD_ROWS, :] = zeros
    upad[ls + PAD_ROWS:ls + 2 * PAD_ROWS, :] = zeros

    def fill(c, carry):
        r0 = pl.multiple_of(c * ch, ch)
        a = a_ref[pl.ds(r0, ch), :].astype(F32)
        vpad[pl.ds(r0 + PAD_ROWS, ch), :] = a[:, :CONV_CH] * _sigmoid(a[:, CONV_CH:])
        upad[pl.ds(r0 + PAD_ROWS, ch), :] = p_ref[pl.ds(r0, ch), :].astype(F32)
        return carry

    lax.fori_loop(0, ls // ch, fill, 0)

    cw = cw_ref[...]
    cb, lg, lb = cb_ref[...], lg_ref[...], lb_ref[...]

    def conv(c, carry):
        r0 = pl.multiple_of(c * CONV_ROWS, CONV_ROWS)
        win = vpad[pl.ds(r0, CONV_ROWS + 2 * PAD_ROWS), :]
        acc = jnp.zeros((CONV_ROWS, CONV_CH), F32)
        for r in range(SUBLANES):
            wr = win[r:r + CONV_ROWS + 24, :]
            for a in range(4):
                k = SUBLANES * a + r - 1
                if 0 <= k < CONV_K:
                    acc = acc + wr[SUBLANES * a:SUBLANES * a + CONV_ROWS, :] * cw[k:k + 1, :]
        y = _ln(acc + cb) * lg + lb
        ya_ref[pl.ds(r0, CONV_ROWS), :] = (y * _sigmoid(y)).astype(BF16)
        return carry

    lax.fori_loop(0, ls // CONV_ROWS, conv, 0)

    ps = ps_ref[...]

    def pool(c, carry):
        r0 = pl.multiple_of(c * ch, ch)
        t = r0 + lax.broadcasted_iota(jnp.int32, (ch, LANES), 0)
        for g, w in enumerate(POOL_WINDOWS):
            lanes = slice(LANES * g, LANES * (g + 1))
            uwin = upad[pl.ds(r0, ch + 2 * PAD_ROWS), lanes]
            s = None
            for o in range(PAD_ROWS - w // 2, PAD_ROWS + w // 2):
                piece = uwin[o:o + ch, :]
                s = piece if s is None else s + piece
            lo = jnp.maximum(t - w // 2, 0)
            hi = jnp.minimum(t + w // 2, ls)
            d = s / (hi - lo).astype(F32) - uwin[PAD_ROWS:PAD_ROWS + ch, :]
            y = jnp.dot(d.astype(BF16), pw_ref[g], preferred_element_type=F32) * ps[:, lanes]
            yc_ref[pl.ds(r0, ch), lanes] = y.astype(BF16)
        return carry

    lax.fori_loop(0, ls // ch, pool, 0)


def _local_mixers(a, p, lw, ls, blk0, nseq):
    seq_in = lambda w: pl.BlockSpec((ls, w), lambda b: (blk0 + b, 0), pipeline_mode=pl.Buffered(1))
    seq_out = pl.BlockSpec((ls, 512), lambda b: (b, 0))
    full = lambda shape: pl.BlockSpec(shape, lambda b: tuple(0 for _ in shape))
    return pl.pallas_call(
        functools.partial(_local_kernel, ls=ls),
        grid=(nseq,),
        in_specs=[seq_in(1024), seq_in(512), full((32, CONV_CH)), full((1, CONV_CH)), full((1, CONV_CH)),
                  full((1, CONV_CH)), full((4, POOL_GROUP, POOL_GROUP)), full((1, POOL_CH))],
        out_specs=[seq_out, seq_out],
        out_shape=[jax.ShapeDtypeStruct((nseq * ls, 512), BF16)] * 2,
        scratch_shapes=[pltpu.VMEM((ls + 2 * PAD_ROWS, CONV_CH), F32),
                        pltpu.VMEM((ls + 2 * PAD_ROWS, POOL_CH), F32)],
        compiler_params=_cparams(("parallel",)),
        name="local_mixers",
    )(a, p, lw["conv_w"], lw["conv_b"], lw["conv_ln_g"], lw["conv_ln_b"], lw["pool_w"], lw["pool_scale"])


def _swa_heads(q, k2, v2, ok, sink_ref, o_ref):
    bq = q.shape[0]
    lane = lax.broadcasted_iota(jnp.int32, (bq, LANES), 1)
    low = lane < 64
    zero = jnp.zeros((), BF16)
    dn = (((1,), (1,)), ((), ()))
    for j in range(SWA_HEADS // 2):
        g = j // 2
        lo_sl = slice(0, LANES) if g == 0 else slice(LANES, 2 * LANES)
        hi_sl = slice(LANES, 2 * LANES) if g == 0 else slice(0, LANES)
        q2 = q[:, LANES * j:LANES * (j + 1)]
        outs = []
        for e in range(2):
            sl = lo_sl if e == 0 else hi_sl
            qm = jnp.where(low if e == 0 else ~low, q2, zero)
            s = lax.dot_general(qm, k2[:, sl], dn, preferred_element_type=F32)
            if ok is not None:
                s = jnp.where(ok, s, MASK_VALUE)
            sk = sink_ref[2 * j + e:2 * j + e + 1, 0:1]
            m = jnp.maximum(jnp.max(s, axis=-1, keepdims=True), sk)
            pexp = jnp.exp(s - m)
            den = jnp.sum(pexp, axis=-1, keepdims=True) + jnp.exp(sk - m)
            o = jnp.dot(pexp.astype(BF16), v2[:, sl], preferred_element_type=F32)
            outs.append(o / den)
        o_ref[:, LANES * j:LANES * (j + 1)] = jnp.where(low, outs[0], outs[1]).astype(BF16)


def _swa_lat_kernel(q_ref, kp_ref, kc_ref, kn_ref, vp_ref, vc_ref, vn_ref, kx_ref, vx_ref, sink_ref,
                    o_ref, *, seq_len):
    i = pl.program_id(1)
    k2 = jnp.concatenate([kp_ref[...], kc_ref[...], kn_ref[...], kx_ref[...]], axis=0)
    v2 = jnp.concatenate([vp_ref[...], vc_ref[...], vn_ref[...], vx_ref[...]], axis=0)
    nk = k2.shape[0]
    r = lax.broadcasted_iota(jnp.int32, (SWA_BLOCK, nk), 0)
    s = lax.broadcasted_iota(jnp.int32, (SWA_BLOCK, nk), 1)
    kpos = (i - 1) * SWA_BLOCK + s
    diff = kpos - (i * SWA_BLOCK + r)
    ok = ((kpos >= 0) & (kpos < seq_len) & (jnp.abs(diff) <= SWA_WINDOW)) | (s >= 3 * SWA_BLOCK)
    _swa_heads(q_ref[...], k2, v2, ok, sink_ref, o_ref)


def _swa_ctx_kernel(q_ref, kx_ref, vx_ref, sink_ref, o_ref):
    _swa_heads(q_ref[...], kx_ref[...], vx_ref[...], None, sink_ref, o_ref)


def _swa(q, k2, v2, sink_b, geo, with_ctx):
    nb, ls, lc = geo["B"], geo["L"], geo["C"]
    nq = ls // SWA_BLOCK
    ctx_blk0 = nb * ls // lc
    qspec = pl.BlockSpec((SWA_BLOCK, 512), lambda b, i: (b * nq + i, 0))
    kv = lambda off: pl.BlockSpec(
        (SWA_BLOCK, 256), lambda b, i: (b * nq + jnp.clip(i + off, 0, nq - 1), 0))
    cx = pl.BlockSpec((lc, 256), lambda b, i: (ctx_blk0 + b, 0))
    sk = pl.BlockSpec((SUBLANES, LANES), lambda b, i: (0, 0))
    yb = pl.pallas_call(
        functools.partial(_swa_lat_kernel, seq_len=ls),
        grid=(nb, nq),
        in_specs=[qspec, kv(-1), kv(0), kv(1), kv(-1), kv(0), kv(1), cx, cx, sk],
        out_specs=qspec,
        out_shape=jax.ShapeDtypeStruct((nb * ls, 512), BF16),
        compiler_params=_cparams(("parallel", "parallel")),
        name="swa_latent",
    )(q, k2, k2, k2, v2, v2, v2, k2, v2, sink_b)
    if not with_ctx:
        return yb, None
    cx1 = pl.BlockSpec((lc, 256), lambda b: (ctx_blk0 + b, 0))
    yb_ctx = pl.pallas_call(
        _swa_ctx_kernel,
        grid=(nb,),
        in_specs=[pl.BlockSpec((lc, 512), lambda b: (ctx_blk0 + b, 0)), cx1, cx1,
                  pl.BlockSpec((SUBLANES, LANES), lambda b: (0, 0))],
        out_specs=pl.BlockSpec((lc, 512), lambda b: (b, 0)),
        out_shape=jax.ShapeDtypeStruct((nb * lc, 512), BF16),
        compiler_params=_cparams(("parallel",)),
        name="swa_context",
    )(q, k2, v2, sink_b)
    return yb, yb_ctx


MLA_KC = 256
MLA_ROWS = MLA_HEADS * MLA_BLOCK
MLA_SOFTMAX_ROWS = 64
LOG2E = 1.4426950408889634


def _mla_kernel(cq_ref, *rest, n_lat, n_ctx):
    if n_lat:
        kl_ref, rest = rest[0], rest[1:]
    (kx_ref, qg_ref, wuq_ref, wukp_ref, wuvp_ref, cs_ref, sn_ref, o_ref,
     q_sc, s_a, s_b, p_a, p_b, al_a, al_b, m_sc, acc_sc) = rest
    kc = MLA_KC
    n = n_lat + n_ctx
    scale = (MLA_NOPE + MLA_ROPE) ** -0.5 * LOG2E
    cq = cq_ref[...].astype(F32)
    cqn = cq * lax.rsqrt(jnp.mean(cq * cq, axis=-1, keepdims=True) + LN_EPS) * qg_ref[...]
    q = jnp.dot(cqn.astype(BF16), wuq_ref[...], preferred_element_type=F32)
    qr = q[:, 512:768]
    lane2 = lax.broadcasted_iota(jnp.int32, (MLA_BLOCK, 2 * LANES), 1)
    rot = jnp.where((lane2 % 32) < 16, pltpu.roll(qr, 2 * LANES - 16, 1), pltpu.roll(qr, 16, 1))
    qr = (qr * cs_ref[...] + rot * sn_ref[...]) * scale
    slot_of_lane = lax.broadcasted_iota(jnp.int32, (MLA_BLOCK, LANES), 1) // 32
    for j in range(MLA_HEADS // 2):
        qa2 = jnp.dot(q[:, LANES * j:LANES * (j + 1)].astype(BF16), wukp_ref[j],
                      preferred_element_type=F32) * scale
        for e in range(2):
            h = 2 * j + e
            rows = slice(h * MLA_BLOCK, (h + 1) * MLA_BLOCK)
            qrh = jnp.where(slot_of_lane == (h % 4), qr[:, LANES * (h // 4):LANES * (h // 4 + 1)], 0.0)
            q_sc[rows, 0:LANES] = qa2[:, LANES * e:LANES * (e + 1)].astype(BF16)
            q_sc[rows, LANES:2 * LANES] = qrh.astype(BF16)

    m_sc[...] = jnp.full(m_sc.shape, MASK_VALUE, F32)
    acc_sc[...] = jnp.zeros(acc_sc.shape, F32)
    dn = (((1,), (1,)), ((), ()))
    s_bufs, p_bufs, al_bufs = (s_a, s_b), (p_a, p_b), (al_a, al_b)
    ones = jnp.ones((kc, LANES), BF16)

    def keys(c):
        if isinstance(c, int):
            if c >= n_lat:
                return kx_ref[(c - n_lat) * kc:(c - n_lat + 1) * kc, :]
            return kl_ref[c * kc:(c + 1) * kc, :]
        return kl_ref[pl.ds(pl.multiple_of(c * kc, kc), kc), :]

    def scores(c, par):
        s_bufs[par][...] = lax.dot_general(q_sc[...], keys(c), dn, preferred_element_type=F32)

    def softmax(par):
        s_ref, p_ref, al_ref = s_bufs[par], p_bufs[par], al_bufs[par]
        for rb in range(MLA_ROWS // MLA_SOFTMAX_ROWS):
            rows = slice(rb * MLA_SOFTMAX_ROWS, (rb + 1) * MLA_SOFTMAX_ROWS)
            sv = s_ref[rows, :]
            m_old = m_sc[rows, :]
            m_new = jnp.maximum(m_old, jnp.max(sv, axis=-1, keepdims=True))
            al_ref[rows, :] = jnp.exp2(m_old - m_new)
            m_sc[rows, :] = m_new
            for t in range(kc // LANES):
                lanes = slice(LANES * t, LANES * (t + 1))
                p_ref[rows, lanes] = jnp.exp2(sv[:, lanes] - m_new).astype(BF16)

    def values(c, par):
        v_aug = jnp.concatenate([keys(c)[:, 0:LANES], ones], axis=1)
        pv = jnp.dot(p_bufs[par][...], v_aug, preferred_element_type=F32)
        al = al_bufs[par][...]
        acc_sc[...] = jnp.concatenate([al, al], axis=1) * acc_sc[...] + pv

    def step(t, par):
        static = isinstance(t, int)
        if not static or t < n:
            scores(t, par)
        if not static or 1 <= t <= n:
            softmax(1 - par)
        if not static or 2 <= t <= n + 1:
            values(t - 2, par)

    loop_lo = 2
    loop_hi = max(loop_lo, n_lat)
    n_pairs = (loop_hi - loop_lo) // 2
    for t in range(0, loop_lo):
        step(t, t % 2)
    if n_pairs:
        def pair(jp, carry):
            t0 = loop_lo + 2 * jp
            step(t0, 0)
            step(t0 + 1, 1)
            return carry
        lax.fori_loop(0, n_pairs, pair, 0)
    for t in range(loop_lo + 2 * n_pairs, n + 2):
        step(t, t % 2)

    acc = acc_sc[...]
    o = acc[:, 0:LANES] / acc[:, LANES:2 * LANES]
    for j in range(MLA_HEADS // 2):
        o2 = jnp.concatenate([o[(2 * j) * MLA_BLOCK:(2 * j + 1) * MLA_BLOCK, :],
                              o[(2 * j + 1) * MLA_BLOCK:(2 * j + 2) * MLA_BLOCK, :]], axis=1)
        o_ref[:, LANES * j:LANES * (j + 1)] = jnp.dot(
            o2.astype(BF16), wuvp_ref[j], preferred_element_type=F32).astype(BF16)


def _mla(cq, kc_all, lw, tabs, geo, lat):
    nb, ls, lc = geo["B"], geo["L"], geo["C"]
    ctx_blk0 = nb * ls // lc
    if lat:
        nq = ls // MLA_BLOCK
        qmap = lambda b, i: (b * nq + i, 0)
        tmap = lambda b, i: (i, 0)
        n_lat = ls // MLA_KC
    else:
        nq = lc // MLA_BLOCK
        qmap = lambda b, i: (nb * (ls // MLA_BLOCK) + b * nq + i, 0)
        tmap = lambda b, i: (ls // MLA_BLOCK, 0)
        n_lat = 0
    omap = lambda b, i: (b * nq + i, 0)
    full = lambda shape: pl.BlockSpec(shape, lambda b, i: tuple(0 for _ in shape))
    in_specs = [pl.BlockSpec((MLA_BLOCK, 256), qmap)]
    args = [cq]
    if lat:
        in_specs.append(pl.BlockSpec((ls, 256), lambda b, i: (b, 0)))
        args.append(kc_all)
    in_specs += [pl.BlockSpec((lc, 256), lambda b, i: (ctx_blk0 + b, 0)),
                 full((1, 256)), full((256, 768)), full((4, LANES, 256)), full((4, 256, LANES)),
                 pl.BlockSpec((MLA_BLOCK, 256), tmap), pl.BlockSpec((MLA_BLOCK, 256), tmap)]
    args += [kc_all, lw["mla_q_g"], lw["wuq"], lw["wukp"], lw["wuvp"], tabs["cq"], tabs["sq"]]
    return pl.pallas_call(
        functools.partial(_mla_kernel, n_lat=n_lat, n_ctx=lc // MLA_KC),
        grid=(nb, nq),
        in_specs=in_specs,
        out_specs=pl.BlockSpec((MLA_BLOCK, 512), omap),
        out_shape=jax.ShapeDtypeStruct((nb * nq * MLA_BLOCK, 512), BF16),
        scratch_shapes=[pltpu.VMEM((MLA_ROWS, 256), BF16),
                        pltpu.VMEM((MLA_ROWS, MLA_KC), F32), pltpu.VMEM((MLA_ROWS, MLA_KC), F32),
                        pltpu.VMEM((MLA_ROWS, MLA_KC), BF16), pltpu.VMEM((MLA_ROWS, MLA_KC), BF16),
                        pltpu.VMEM((MLA_ROWS, LANES), F32), pltpu.VMEM((MLA_ROWS, LANES), F32),
                        pltpu.VMEM((MLA_ROWS, LANES), F32), pltpu.VMEM((MLA_ROWS, 2 * LANES), F32)],
        compiler_params=_cparams(("parallel", "parallel")),
        name="mla_latent" if lat else "mla_context",
    )(*args)


def _merge_kernel(x_ref, *rest, alpha, n_lat_tiles, has_ctx):
    n_y = 2 * N_BRANCH if has_ctx else N_BRANCH
    y_refs, rest = rest[:n_y], rest[n_y:]
    (sh1_ref, sc1_ref, g1_ref, sh2_ref, sc2_ref, wg_ref, bg_ref, wb_ref, wo_ref, l1g_ref, l1b_ref,
     rw_ref, rb_ref, x1_ref, h2_ref, ei_ref, ew_ref) = rest
    tm = x_ref.shape[0]
    is_ctx = pl.program_id(0) >= n_lat_tiles
    x = x_ref[...]
    h = (_ln(x) * (1.0 + sc1_ref[...]) + sh1_ref[...]).astype(BF16)
    merged = None
    for i in range(N_BRANCH):
        yi = y_refs[i][...]
        if has_ctx:
            yi = jnp.where(is_ctx, y_refs[N_BRANCH + i][...], yi)
        gate = _sigmoid(jnp.dot(h, wg_ref[i], preferred_element_type=F32) + bg_ref[i])
        term = gate * jnp.dot(yi, wb_ref[i], preferred_element_type=F32)
        merged = term if merged is None else merged + term
    y = jnp.dot(merged.astype(BF16), wo_ref[...], preferred_element_type=F32)
    x1 = _ln(alpha * x + g1_ref[...] * y) * l1g_ref[...] + l1b_ref[...]
    x1_ref[...] = x1
    h2 = _ln(x1) * (1.0 + sc2_ref[...]) + sh2_ref[...]
    for j in range(ROW_TILES):
        h2_ref[pl.ds(j, tm, stride=ROW_TILES), :] = h2[:, LANES * j:LANES * (j + 1)]
    lg = jnp.dot(h2, rw_ref[...], precision=HIGHEST, preferred_element_type=F32) + rb_ref[...]
    lane = lax.broadcasted_iota(jnp.int32, (tm, LANES), 1).astype(F32)
    vals, idxs = [], []
    for _ in range(TOP_K):
        mx = jnp.max(lg, axis=-1, keepdims=True)
        ix = jnp.min(jnp.where(lg == mx, lane, float(LANES)), axis=-1, keepdims=True)
        vals.append(mx)
        idxs.append(ix)
        lg = jnp.where(lane == ix, -3.0e38, lg)
    exps = [jnp.exp(v - vals[0]) for v in vals]
    den = exps[0] + exps[1] + exps[2] + exps[3]
    ei = jnp.zeros((tm, LANES), F32)
    ew = jnp.zeros((tm, LANES), F32)
    for k in range(TOP_K):
        ei = jnp.where(lane == float(k), idxs[k], ei)
        ew = jnp.where(lane == float(k), exps[k] / den, ew)
    ei_ref[...] = ei.astype(jnp.int32)
    ew_ref[...] = ew


def _merge(x, ys_lat, ys_ctx, mods, lw, geo, alpha):
    tm, n_lat_tiles, tps, nb = geo["tm"], geo["n_lat_tiles"], geo["tps"], geo["B"]
    d = x.shape[1]
    has_ctx = ys_ctx is not None
    n_tiles = geo["n_tiles"] if has_ctx else n_lat_tiles
    n_ctx_tiles = geo["n_tiles"] - n_lat_tiles
    t_rows = n_tiles * tm

    def cond(i):
        return jnp.where(i < n_lat_tiles, i // tps, nb)

    row = lambda w: pl.BlockSpec((tm, w), lambda i: (i, 0))
    y_lat = pl.BlockSpec((tm, BRANCH_W), lambda i: (jnp.minimum(i, n_lat_tiles - 1), 0))
    y_ctx = pl.BlockSpec((tm, BRANCH_W), lambda i: (jnp.clip(i - n_lat_tiles, 0, n_ctx_tiles - 1), 0))
    y_specs = [y_lat] * N_BRANCH + ([y_ctx] * N_BRANCH if has_ctx else [])
    ys = tuple(ys_lat) + (tuple(ys_ctx) if has_ctx else ())
    mod = pl.BlockSpec((None, 1, d), lambda i: (cond(i), 0, 0))
    full = lambda shape: pl.BlockSpec(shape, lambda i: tuple(0 for _ in shape), pipeline_mode=pl.Buffered(1))
    return pl.pallas_call(
        functools.partial(_merge_kernel, alpha=alpha, n_lat_tiles=n_lat_tiles, has_ctx=has_ctx),
        grid=(n_tiles,),
        name="merge_router",
        in_specs=[row(d)] + y_specs + [mod, mod, mod, mod, mod,
                  full((4, d, d)), full((4, 1, d)), full((4, BRANCH_W, d)), full((d, d)),
                  full((1, d)), full((1, d)), full((d, LANES)), full((1, LANES))],
        out_specs=[row(d), pl.BlockSpec((tm * ROW_TILES, LANES), lambda i: (i, 0)), row(LANES), row(LANES)],
        out_shape=[jax.ShapeDtypeStruct((t_rows, d), F32),
                   jax.ShapeDtypeStruct((t_rows * ROW_TILES, LANES), F32),
                   jax.ShapeDtypeStruct((t_rows, LANES), jnp.int32),
                   jax.ShapeDtypeStruct((t_rows, LANES), F32)],
        compiler_params=_cparams(("parallel",)),
    )(x, *ys, mods["sh1"], mods["sc1"], mods["g1"], mods["sh2"], mods["sc2"],
      lw["w_gate"], lw["b_gate"], lw["w_branch"], lw["w_out"], lw["ln1_g"], lw["ln1_b"],
      lw["router_w"], lw["router_b"])


GMM_IDX_SLOTS = 8
GMM_X_SLOTS = 3


def _gmm_kernel(blk_e_ref, nvalid_ref, idx_hbm, h2_hbm, wgu_ref, bgu_ref, wdn_ref, bdn_ref,
                ytk_hbm, idx_sm, xbuf, ybuf, sem_i, sem_g, sem_s):
    del blk_e_ref
    b = pl.program_id(0)
    n_blocks = pl.num_programs(0)
    nvalid = nvalid_ref[0]
    bm = xbuf.shape[1] // ROW_TILES
    rows = bm * ROW_TILES
    pad0 = ytk_hbm.shape[0] - rows

    def idx_copy(blk):
        slot = blk % GMM_IDX_SLOTS
        return pltpu.make_async_copy(idx_hbm.at[jnp.minimum(blk, n_blocks - 1)], idx_sm.at[slot], sem_i.at[slot])

    def gather_row(blk, r):
        src = pl.multiple_of(idx_sm[blk % GMM_IDX_SLOTS, 0, r] * ROW_TILES, ROW_TILES)
        slot = blk % GMM_X_SLOTS
        pltpu.make_async_copy(h2_hbm.at[pl.ds(src, ROW_TILES), :],
                              xbuf.at[slot, pl.ds(r * ROW_TILES, ROW_TILES), :], sem_g.at[slot]).start()

    def scatter_row(blk, r):
        dst = pl.multiple_of(idx_sm[(blk + GMM_IDX_SLOTS) % GMM_IDX_SLOTS, 1, r] * ROW_TILES, ROW_TILES)
        slot = (blk + 2) % 2
        pltpu.make_async_copy(ybuf.at[slot, pl.ds(r * ROW_TILES, ROW_TILES), :],
                              ytk_hbm.at[pl.ds(dst, ROW_TILES), :], sem_s.at[slot]).start()

    def gather_wait(blk):
        slot = blk % GMM_X_SLOTS
        pltpu.make_async_copy(h2_hbm.at[pl.ds(0, rows), :], xbuf.at[slot], sem_g.at[slot]).wait()

    def scatter_wait(blk):
        slot = (blk + 2) % 2
        pltpu.make_async_copy(ybuf.at[slot], ytk_hbm.at[pl.ds(0, rows), :], sem_s.at[slot]).wait()

    def rolled(fn):
        def body(r, carry):
            fn(r)
            return carry
        lax.fori_loop(0, bm, body, 0)

    @pl.when(b == 0)
    def _():
        ybuf[1] = jnp.zeros(ybuf.shape[1:], F32)

        def fill(r):
            idx_sm[GMM_IDX_SLOTS - 1, 1, r] = pad0 // ROW_TILES + r

        rolled(fill)
        for blk in (0, 1):
            idx_copy(blk).start()
            idx_copy(blk).wait()
        idx_copy(2).start()
        rolled(lambda r: gather_row(0, r))
        rolled(lambda r: gather_row(1, r))

    @pl.when(b < nvalid)
    def _():
        par = b % 2
        idx_copy(b + 3).start()
        idx_copy(b + 2).wait()
        gather_wait(b)
        xs = jnp.concatenate(
            [xbuf[b % GMM_X_SLOTS, pl.ds(j, bm, stride=ROW_TILES), :] for j in range(ROW_TILES)],
            axis=1).astype(BF16)
        for r in range(bm):
            gather_row(b + 2, r)
        for r in range(bm):
            scatter_row(b - 1, r)
        z = jnp.dot(xs, wgu_ref[...], preferred_element_type=F32) + bgu_ref[...]
        gate = jnp.minimum(z[:, :D_EXPERT], SWIGLU_LIMIT)
        up = jnp.clip(z[:, D_EXPERT:], -SWIGLU_LIMIT, SWIGLU_LIMIT)
        act = (up + 1.0) * gate * _sigmoid(SWIGLU_ALPHA * gate)
        y = jnp.dot(act.astype(BF16), wdn_ref[...], preferred_element_type=F32) + bdn_ref[...]
        for j in range(ROW_TILES):
            ybuf[par, pl.ds(j, bm, stride=ROW_TILES), :] = y[:, LANES * j:LANES * (j + 1)]
        scatter_wait(b - 1)

    @pl.when(b == nvalid - 1)
    def _():
        rolled(lambda r: scatter_row(b, r))
        scatter_wait(b)
        gather_wait(b + 1)
        gather_wait(b + 2)
        idx_copy(b + 3).wait()


def _gmm(h2g, plan, lw, n_tok):
    bm = EXPERT_BM
    n_blocks = plan["idx"].shape[0]
    d = lw["w_gu"].shape[1]
    n_slots = n_tok * TOP_K + bm
    grid_spec = pltpu.PrefetchScalarGridSpec(
        num_scalar_prefetch=2,
        grid=(n_blocks,),
        in_specs=[
            pl.BlockSpec(memory_space=pl.ANY),
            pl.BlockSpec(memory_space=pl.ANY),
            pl.BlockSpec((None, d, 2 * D_EXPERT), lambda b, be, nv: (be[b], 0, 0)),
            pl.BlockSpec((None, 1, 2 * D_EXPERT), lambda b, be, nv: (be[b], 0, 0)),
            pl.BlockSpec((None, D_EXPERT, d), lambda b, be, nv: (be[b], 0, 0)),
            pl.BlockSpec((None, 1, d), lambda b, be, nv: (be[b], 0, 0)),
        ],
        out_specs=pl.BlockSpec(memory_space=pl.ANY),
        scratch_shapes=[pltpu.SMEM((GMM_IDX_SLOTS, 2, bm), jnp.int32),
                        pltpu.VMEM((GMM_X_SLOTS, bm * ROW_TILES, LANES), F32),
                        pltpu.VMEM((2, bm * ROW_TILES, LANES), F32),
                        pltpu.SemaphoreType.DMA((GMM_IDX_SLOTS,)),
                        pltpu.SemaphoreType.DMA((GMM_X_SLOTS,)),
                        pltpu.SemaphoreType.DMA((2,))],
    )
    return pl.pallas_call(
        _gmm_kernel,
        grid_spec=grid_spec,
        out_shape=jax.ShapeDtypeStruct((n_slots * ROW_TILES, LANES), F32),
        compiler_params=_cparams(("arbitrary",)),
        name="expert_gmm",
    )(plan["blk_e"], plan["nvalid"], plan["idx"], h2g, lw["w_gu"], lw["b_gu"], lw["w_down"], lw["b_down"])


def _plan(ei, n_tok):
    bm = EXPERT_BM
    n_assign = n_tok * TOP_K
    flat_e = ei.reshape(-1)
    order = jnp.argsort(flat_e, stable=True).astype(jnp.int32)
    experts = jnp.arange(N_EXPERTS, dtype=jnp.int32)
    counts = jnp.sum((flat_e[:, None] == experts[None, :]).astype(jnp.int32), axis=0)
    padded = (counts + bm - 1) // bm * bm
    pad_end = jnp.cumsum(padded)
    pad_start = pad_end - padded
    grp_start = jnp.cumsum(counts) - counts
    n_blocks = -(-(n_assign + N_EXPERTS * (bm - 1)) // bm)
    blk_first = jnp.arange(n_blocks, dtype=jnp.int32) * bm
    blk_e = jnp.minimum(jnp.sum((pad_end[None, :] <= blk_first[:, None]).astype(jnp.int32), axis=1),
                        N_EXPERTS - 1)
    within = blk_first[:, None] + jnp.arange(bm, dtype=jnp.int32)[None, :] - pad_start[blk_e][:, None]
    valid = (within < counts[blk_e][:, None]) & (blk_first < pad_end[-1])[:, None]
    pos = jnp.clip(grp_start[blk_e][:, None] + within, 0, n_assign - 1)
    row_asg = order[pos]
    row_src = jnp.where(valid, row_asg // TOP_K, 0)
    row_dst = jnp.where(valid, (row_asg % TOP_K) * n_tok + row_asg // TOP_K,
                        n_assign + jnp.arange(bm, dtype=jnp.int32)[None, :])
    idx = jnp.stack([row_src, row_dst], axis=1).astype(jnp.int32)
    nvalid = (pad_end[-1:] // bm).astype(jnp.int32)
    return {"idx": idx, "blk_e": blk_e.astype(jnp.int32), "nvalid": nvalid}


def _combine_kernel(x_ref, y0_ref, y1_ref, y2_ref, y3_ref, ew_ref, g2_ref, lg_ref, lb_ref, o_ref, *, alpha):
    tm = x_ref.shape[0]
    ew = ew_ref[...]
    cols = []
    for j in range(ROW_TILES):
        acc = None
        for k, y_ref in enumerate((y0_ref, y1_ref, y2_ref, y3_ref)):
            piece = y_ref[pl.ds(j, tm, stride=ROW_TILES), :] * ew[:, k:k + 1]
            acc = piece if acc is None else acc + piece
        cols.append(acc)
    m = jnp.concatenate(cols, axis=1)
    o_ref[...] = _ln(alpha * x_ref[...] + g2_ref[...] * m) * lg_ref[...] + lb_ref[...]


def _combine(x1, ytk, ew, g2, lw, geo, n_tiles, alpha):
    tm, n_lat_tiles, tps, nb = geo["tmc"], geo["n_lat_tiles"], geo["tps"], geo["B"]
    ratio = geo["tm"] // tm
    d = x1.shape[1]

    def cond(i):
        return jnp.where(i < n_lat_tiles * ratio, i // (tps * ratio), nb)

    def slot_map(k, i):
        return (k * n_tiles * ratio + i, 0)

    row = lambda w: pl.BlockSpec((tm, w), lambda i: (i, 0))
    full = lambda shape: pl.BlockSpec(shape, lambda i: tuple(0 for _ in shape))
    return pl.pallas_call(
        functools.partial(_combine_kernel, alpha=alpha),
        grid=(n_tiles * ratio,),
        in_specs=[row(d)] + [pl.BlockSpec((tm * ROW_TILES, LANES), functools.partial(slot_map, k))
                             for k in range(TOP_K)] + [row(LANES),
                  pl.BlockSpec((None, 1, d), lambda i: (cond(i), 0, 0)), full((1, d)), full((1, d))],
        out_specs=row(d),
        out_shape=jax.ShapeDtypeStruct((n_tiles * tm * ratio, d), F32),
        compiler_params=_cparams(("parallel",)),
        name="moe_combine",
    )(x1, ytk, ytk, ytk, ytk, ew, g2, lw["ln2_g"], lw["ln2_b"])


def _rope_tables(seq_len, tm):
    rows = seq_len // GRID_W
    row = jnp.repeat(jnp.arange(rows, dtype=F32), GRID_W)
    col = jnp.tile(jnp.arange(GRID_W, dtype=F32), rows)

    def table(rot_dim, reps, pad_rows):
        n_freq = rot_dim // 4
        inv = ROPE_THETA ** (-jnp.arange(n_freq, dtype=F32) / n_freq)
        ang = jnp.concatenate([row[:, None] * inv, col[:, None] * inv], axis=-1)
        cos, sin = jnp.cos(ang), jnp.sin(ang)
        c = jnp.tile(jnp.concatenate([cos, cos], axis=-1), (1, reps))
        s = jnp.tile(jnp.concatenate([-sin, sin], axis=-1), (1, reps))
        c = jnp.concatenate([c, jnp.ones((pad_rows, c.shape[1]), F32)], axis=0)
        s = jnp.concatenate([s, jnp.zeros((pad_rows, s.shape[1]), F32)], axis=0)
        return c, s

    cs, sn = table(HEAD_DIM, 2, tm)
    ck, sk = table(MLA_ROPE, 4, tm)
    cq, sq = table(MLA_ROPE, 8, MLA_BLOCK)
    return {"cs": cs, "sn": sn, "ck": ck, "sk": sk, "cq": cq, "sq": sq}


def _layer_weights(l, w_in, conv_w, conv_b, conv_ln_g, conv_ln_b, swa_sink, pool_w, pool_scale, mla_q_g,
                   mla_w_uq, mla_kv_g, mla_w_uk, mla_w_uv, w_branch, w_gate, b_gate, w_out, ln1_g, ln1_b,
                   router_w, router_b, w_gu_bf, b_gu, w_down_bf, b_down, ln2_g, ln2_b):
    d = w_in.shape[1]
    wi = w_in[l]
    kr = wi[:, 2688:2720]
    w_all = jnp.concatenate([wi[:, :2688], kr, kr, kr, kr], axis=1).astype(BF16)
    wuq = mla_w_uq[l].reshape(MLA_Q_RANK, MLA_HEADS, MLA_NOPE + MLA_ROPE)
    wuq = jnp.concatenate([wuq[:, :, :MLA_NOPE].reshape(MLA_Q_RANK, -1),
                           wuq[:, :, MLA_NOPE:].reshape(MLA_Q_RANK, -1)], axis=1).astype(BF16)
    wuk = jnp.transpose(mla_w_uk[l], (1, 2, 0))
    wuv = jnp.transpose(mla_w_uv[l], (1, 0, 2))
    zk = jnp.zeros((MLA_NOPE, MLA_KV_RANK), F32)
    zv = jnp.zeros((MLA_KV_RANK, MLA_V), F32)
    wukp = jnp.stack([jnp.block([[wuk[2 * j], zk], [zk, wuk[2 * j + 1]]]) for j in range(MLA_HEADS // 2)])
    wuvp = jnp.stack([jnp.block([[wuv[2 * j], zv], [zv, wuv[2 * j + 1]]]) for j in range(MLA_HEADS // 2)])
    rw = jnp.zeros((d, LANES), F32).at[:, :N_EXPERTS].set(router_w[l])
    rb = jnp.full((1, LANES), MASK_VALUE, F32).at[0, :N_EXPERTS].set(router_b[l])
    return {
        "w_all": w_all,
        "conv_w": jnp.concatenate([conv_w[l], jnp.zeros((1, CONV_CH), F32)], axis=0),
        "conv_b": conv_b[l][None], "conv_ln_g": conv_ln_g[l][None], "conv_ln_b": conv_ln_b[l][None],
        "sink": jnp.broadcast_to(swa_sink[l][:, None], (SWA_HEADS, LANES)),
        "pool_w": pool_w[l].astype(BF16), "pool_scale": pool_scale[l][None],
        "mla_q_g": mla_q_g[l][None], "mla_kv_g": mla_kv_g[l][None],
        "wuq": wuq, "wukp": wukp.astype(BF16), "wuvp": wuvp.astype(BF16),
        "w_branch": w_branch[l].astype(BF16), "w_gate": w_gate[l].astype(BF16),
        "b_gate": b_gate[l][:, None, :], "w_out": w_out[l].astype(BF16),
        "ln1_g": ln1_g[l][None], "ln1_b": ln1_b[l][None],
        "router_w": rw, "router_b": rb,
        "w_gu": w_gu_bf[l], "b_gu": b_gu[l][:, None, :],
        "w_down": w_down_bf[l], "b_down": b_down[l][:, None, :],
        "ln2_g": ln2_g[l][None], "ln2_b": ln2_b[l][None],
    }


def _token_tile(seq_len, ctx_rows):
    for tm in (512, 256, 128):
        if seq_len % tm == 0 and ctx_rows % tm == 0:
            return tm
    raise ValueError("unsupported sequence / context lengths")


def kernel(x, c, ctx, c_ctx, w_ada, b_ada, w_in, conv_w, conv_b, conv_ln_g, conv_ln_b, swa_sink, pool_w,
           pool_scale, mla_q_g, mla_w_uq, mla_kv_g, mla_w_uk, mla_w_uv, w_branch, w_gate, b_gate, w_out,
           ln1_g, ln1_b, router_w, router_b, w_gu, b_gu, w_down, b_down, ln2_g, ln2_b):
    nb, ls, d = x.shape
    lc = ctx.shape[1]
    depth = w_ada.shape[0]
    assert ls % GRID_W == 0 and ls % SWA_BLOCK == 0 and (nb * ls) % lc == 0
    assert ls % MLA_KC == 0 and lc % MLA_KC == 0 and lc % MLA_BLOCK == 0
    tm = _token_tile(ls, nb * lc)
    t_lat, t_ctx = nb * ls, nb * lc
    geo = {"B": nb, "L": ls, "C": lc, "tm": tm, "tmc": min(tm, 256), "tps": ls // tm,
           "n_lat_tiles": t_lat // tm, "n_tiles": (t_lat + t_ctx) // tm}
    alpha = (2 * depth) ** 0.25

    ra = -(-(nb + 1) // SUBLANES) * SUBLANES
    cvec = jnp.zeros((ra, d), F32).at[:nb].set(c).at[nb].set(c_ctx)
    ada = _ada_terms(cvec, w_ada, b_ada).reshape(depth, ra, 6, 1, d)
    tabs = _rope_tables(ls, tm)
    w_gu_bf = w_gu.astype(BF16)
    w_down_bf = w_down.astype(BF16)

    xa = jnp.concatenate([x.reshape(t_lat, d), ctx.reshape(t_ctx, d)], axis=0)
    for l in range(depth):
        ctx_out = l < depth - 1
        n_tiles = geo["n_tiles"] if ctx_out else geo["n_lat_tiles"]
        names = ("sh1", "sc1", "g1", "sh2", "sc2", "g2")
        mods = {n: ada[l, :, i] for i, n in enumerate(names)}
        lw = _layer_weights(l, w_in, conv_w, conv_b, conv_ln_g, conv_ln_b, swa_sink, pool_w, pool_scale,
                            mla_q_g, mla_w_uq, mla_kv_g, mla_w_uk, mla_w_uv, w_branch, w_gate, b_gate,
                            w_out, ln1_g, ln1_b, router_w, router_b, w_gu_bf, b_gu, w_down_bf, b_down,
                            ln2_g, ln2_b)
        a, q, k2, v2, p, cq, kc = _in_proj(xa, mods["sh1"], mods["sc1"], lw["w_all"], tabs,
                                           lw["mla_kv_g"], geo)
        ya, yc = _local_mixers(a, p, lw, ls, 0, nb)
        yb, yb_ctx = _swa(q, k2, v2, lw["sink"], geo, ctx_out)
        yd = _mla(cq, kc, lw, tabs, geo, True)
        ys_ctx = None
        if ctx_out:
            ya_ctx, yc_ctx = _local_mixers(a, p, lw, lc, t_lat // lc, nb)
            yd_ctx = _mla(cq, kc, lw, tabs, geo, False)
            ys_ctx = (ya_ctx, yb_ctx, yc_ctx, yd_ctx)
        x1, h2g, ei, ew = _merge(xa, (ya, yb, yc, yd), ys_ctx, mods, lw, geo, alpha)
        n_tok = n_tiles * tm
        plan = _plan(ei[:, :TOP_K], n_tok)
        ytk = _gmm(h2g, plan, lw, n_tok)
        xa = _combine(x1, ytk, ew, mods["g2"], lw, geo, n_tiles, alpha)
    return xa[:t_lat].reshape(nb, ls, d)
```

```python
import functools

import jax
import jax.numpy as jnp
import numpy as np
from jax import lax
from jax.experimental import pallas as pl
from jax.experimental.pallas import tpu as pltpu

F32 = jnp.float32
BF16 = jnp.bfloat16
HIGHEST = lax.Precision.HIGHEST

GRID_W = 64
HEAD_DIM = 64
ROPE_THETA = 10000.0
LN_EPS = 1e-6
MASK_VALUE = -1e30
CONV_CH = 512
CONV_K = 31
SWA_HEADS = 8
SWA_WINDOW = 128
SWA_BLOCK = 128
POOL_CH = 512
POOL_WINDOWS = (2, 4, 8, 16)
POOL_GROUP = 128
MLA_HEADS = 8
MLA_Q_RANK = 256
MLA_KV_RANK = 128
MLA_NOPE = 64
MLA_ROPE = 32
MLA_V = 64
MLA_BLOCK = 128
N_BRANCH = 4
BRANCH_W = 512
N_EXPERTS = 32
TOP_K = 4
D_EXPERT = 1024
SWIGLU_LIMIT = 7.0
SWIGLU_ALPHA = 1.702

LOG2E = 1.4426950408889634

LANES = 128
SUBLANES = 8
ROW_TILES = 8
EXPERT_BM = 512
VMEM_LIMIT = 56 * 1024 * 1024


def _cparams(sem):
    return pltpu.CompilerParams(dimension_semantics=sem, vmem_limit_bytes=VMEM_LIMIT)


def _ln(x):
    mu = jnp.mean(x, axis=-1, keepdims=True)
    xc = x - mu
    var = jnp.mean(xc * xc, axis=-1, keepdims=True)
    return xc * lax.rsqrt(var + LN_EPS)


def _sigmoid(x):
    return 0.5 * jnp.tanh(0.5 * x) + 0.5


def _ada_kernel(c_ref, w_ref, b_ref, o_ref):
    c = c_ref[...]
    s = c * _sigmoid(c)
    o_ref[...] = jnp.dot(s, w_ref[...], precision=HIGHEST, preferred_element_type=F32) + b_ref[...]


def _ada_terms(cvec, w_ada, b_ada):
    depth, d, n = w_ada.shape
    ra = cvec.shape[0]
    tn = 1536
    return pl.pallas_call(
        _ada_kernel,
        grid=(depth, n // tn),
        in_specs=[
            pl.BlockSpec((ra, d), lambda l, j: (0, 0)),
            pl.BlockSpec((None, d, tn), lambda l, j: (l, 0, j)),
            pl.BlockSpec((None, 1, tn), lambda l, j: (l, 0, j)),
        ],
        out_specs=pl.BlockSpec((None, ra, tn), lambda l, j: (l, 0, j)),
        out_shape=jax.ShapeDtypeStruct((depth, ra, n), F32),
        compiler_params=_cparams(("parallel", "parallel")),
        name="ada_terms",
    )(cvec, w_ada, b_ada.reshape(depth, 1, n))


IN_COLS = 1024 + 512 + 128 + 128 + 512 + 256 + 128 + 128


def _in_kernel(x_ref, sh_ref, sc_ref, w_ref, cs_ref, sn_ref, ck_ref, sk_ref, kvg_ref,
               a_ref, q_ref, k2_ref, v2_ref, p_ref, cq_ref, kc_ref):
    tm = x_ref.shape[0]
    h = _ln(x_ref[...]) * (1.0 + sc_ref[...]) + sh_ref[...]
    z = jnp.dot(h.astype(BF16), w_ref[...], preferred_element_type=F32)
    a_ref[...] = z[:, 0:1024].astype(BF16)
    cs, sn = cs_ref[...], sn_ref[...]
    lane = lax.broadcasted_iota(jnp.int32, (tm, LANES), 1)
    first32 = (lane % 64) < 32

    def rope64(t):
        rot = jnp.where(first32, pltpu.roll(t, LANES - 32, 1), pltpu.roll(t, 32, 1))
        return t * cs + rot * sn

    for j in range(4):
        qj = rope64(z[:, 1024 + LANES * j:1024 + LANES * (j + 1)]) * (HEAD_DIM ** -0.5 * LOG2E)
        q_ref[:, LANES * j:LANES * (j + 1)] = qj.astype(BF16)
    low64 = lane < 64
    k = rope64(z[:, 1536:1664])
    ks = pltpu.roll(k, 64, 1)
    k2_ref[:, 0:LANES] = jnp.where(low64, k, ks).astype(BF16)
    k2_ref[:, LANES:2 * LANES] = jnp.where(low64, ks, k).astype(BF16)
    v = z[:, 1664:1792]
    vs = pltpu.roll(v, 64, 1)
    v2_ref[:, 0:LANES] = jnp.where(low64, v, vs).astype(BF16)
    v2_ref[:, LANES:2 * LANES] = jnp.where(low64, vs, v).astype(BF16)
    p_ref[...] = z[:, 1792:2304].astype(BF16)
    cq_ref[...] = z[:, 2304:2560].astype(BF16)
    ckv = z[:, 2560:2688]
    ckvn = ckv * lax.rsqrt(jnp.mean(ckv * ckv, axis=-1, keepdims=True) + LN_EPS) * kvg_ref[...]
    kr4 = z[:, 2688:2816]
    first16 = (lane % 32) < 16
    rot = jnp.where(first16, pltpu.roll(kr4, LANES - 16, 1), pltpu.roll(kr4, 16, 1))
    kr4 = kr4 * ck_ref[...] + rot * sk_ref[...]
    kc_ref[:, 0:LANES] = ckvn.astype(BF16)
    kc_ref[:, LANES:2 * LANES] = kr4.astype(BF16)


def _in_proj(x, sh, sc, w_all, tabs, kvg, geo):
    tm, n_tiles, n_lat_tiles, tps, nb = geo["tm"], geo["n_tiles"], geo["n_lat_tiles"], geo["tps"], geo["B"]
    t_rows, d = n_tiles * tm, x.shape[1]

    def cond(i):
        return jnp.where(i < n_lat_tiles, i // tps, nb)

    def pos(i):
        return jnp.where(i < n_lat_tiles, i % tps, tps)

    row = lambda w: pl.BlockSpec((tm, w), lambda i: (i, 0))
    mod = pl.BlockSpec((None, 1, d), lambda i: (cond(i), 0, 0))
    tab = pl.BlockSpec((tm, LANES), lambda i: (pos(i), 0))
    widths = (1024, 512, 256, 256, 512, 256, 256)
    return pl.pallas_call(
        _in_kernel,
        grid=(n_tiles,),
        in_specs=[row(d), mod, mod,
                  pl.BlockSpec((d, IN_COLS), lambda i: (0, 0)),
                  tab, tab, tab, tab,
                  pl.BlockSpec((1, LANES), lambda i: (0, 0))],
        out_specs=[row(w) for w in widths],
        out_shape=[jax.ShapeDtypeStruct((t_rows, w), BF16) for w in widths],
        compiler_params=_cparams(("parallel",)),
        name="in_proj",
    )(x, sh, sc, w_all, tabs["cs"], tabs["sn"], tabs["ck"], tabs["sk"], kvg)


CONV_ROWS = 32
CONV_UNROLL = 4
PAD_ROWS = 16


def _local_kernel(a_ref, p_ref, cw_ref, sh_ref, cb_ref, lg_ref, lb_ref, pw_ref, ps_ref, ya_ref, yc_ref, vpad, upad,
                  *, ls):
    ch = min(256, ls)
    zeros = jnp.zeros((PAD_ROWS, CONV_CH), F32)
    vpad[0:PAD_ROWS, :] = zeros
    vpad[ls + PAD_ROWS:ls + 2 * PAD_ROWS, :] = zeros
    upad[0:PAD_ROWS, :] = zeros
    upad[ls + PAD_ROWS:ls + 2 * PAD_ROWS, :] = zeros

    def fill(c, carry):
        r0 = pl.multiple_of(c * ch, ch)
        a = a_ref[pl.ds(r0, ch), :].astype(F32)
        vpad[pl.ds(r0 + PAD_ROWS, ch), :] = a[:, :CONV_CH] * _sigmoid(a[:, CONV_CH:])
        upad[pl.ds(r0 + PAD_ROWS, ch), :] = p_ref[pl.ds(r0, ch), :].astype(F32)
        return carry

    lax.fori_loop(0, ls // ch, fill, 0)

    cb, lg, lb = cb_ref[...], lg_ref[...], lb_ref[...]
    groups = CONV_ROWS // SUBLANES

    def conv_chunk(r0):
        win = vpad[pl.ds(r0, CONV_ROWS + 2 * PAD_ROWS), :].astype(BF16)
        accs = [jnp.zeros((SUBLANES, CONV_CH), F32) for _ in range(groups)]
        for r in range(SUBLANES):
            wr = win.astype(F32) if r == 0 else jnp.dot(sh_ref[r], win, preferred_element_type=F32)
            for a in range(4):
                k = SUBLANES * a + r - 1
                if 0 <= k < CONV_K:
                    w8 = cw_ref[SUBLANES * k:SUBLANES * (k + 1), :]
                    for g in range(groups):
                        lo = SUBLANES * (a + g)
                        accs[g] = accs[g] + wr[lo:lo + SUBLANES, :] * w8
        y = _ln(jnp.concatenate(accs, axis=0) + cb) * lg + lb
        ya_ref[pl.ds(r0, CONV_ROWS), :] = (y * _sigmoid(y)).astype(BF16)

    def conv(c, carry):
        for u in range(CONV_UNROLL):
            conv_chunk(pl.multiple_of((c * CONV_UNROLL + u) * CONV_ROWS, CONV_ROWS))
        return carry

    lax.fori_loop(0, ls // (CONV_ROWS * CONV_UNROLL), conv, 0)

    ps = ps_ref[...]

    def pool(c, carry):
        r0 = pl.multiple_of(c * ch, ch)
        t = r0 + lax.broadcasted_iota(jnp.int32, (ch, LANES), 0)
        for g, w in enumerate(POOL_WINDOWS):
            lanes = slice(LANES * g, LANES * (g + 1))
            uwin = upad[pl.ds(r0, ch + 2 * PAD_ROWS), lanes]
            s = None
            for o in range(PAD_ROWS - w // 2, PAD_ROWS + w // 2):
                piece = uwin[o:o + ch, :]
                s = piece if s is None else s + piece
            lo = jnp.maximum(t - w // 2, 0)
            hi = jnp.minimum(t + w // 2, ls)
            d = s / (hi - lo).astype(F32) - uwin[PAD_ROWS:PAD_ROWS + ch, :]
            y = jnp.dot(d.astype(BF16), pw_ref[g], preferred_element_type=F32) * ps[:, lanes]
            yc_ref[pl.ds(r0, ch), lanes] = y.astype(BF16)
        return carry

    lax.fori_loop(0, ls // ch, pool, 0)


def _local_mixers(a, p, lw, ls, blk0, nseq):
    seq_in = lambda w: pl.BlockSpec((ls, w), lambda b: (blk0 + b, 0), pipeline_mode=pl.Buffered(1))
    seq_out = pl.BlockSpec((ls, 512), lambda b: (b, 0))
    full = lambda shape: pl.BlockSpec(shape, lambda b: tuple(0 for _ in shape))
    return pl.pallas_call(
        functools.partial(_local_kernel, ls=ls),
        grid=(nseq,),
        in_specs=[seq_in(1024), seq_in(512), full((SUBLANES * 32, CONV_CH)),
                  full((SUBLANES, CONV_ROWS + 2 * PAD_ROWS, CONV_ROWS + 2 * PAD_ROWS)), full((1, CONV_CH)), full((1, CONV_CH)),
                  full((1, CONV_CH)), full((4, POOL_GROUP, POOL_GROUP)), full((1, POOL_CH))],
        out_specs=[seq_out, seq_out],
        out_shape=[jax.ShapeDtypeStruct((nseq * ls, 512), BF16)] * 2,
        scratch_shapes=[pltpu.VMEM((ls + 2 * PAD_ROWS, CONV_CH), F32),
                        pltpu.VMEM((ls + 2 * PAD_ROWS, POOL_CH), F32)],
        compiler_params=_cparams(("parallel",)),
        name="local_mixers",
    )(a, p, lw["conv_w"], lw["conv_shift"], lw["conv_b"], lw["conv_ln_g"], lw["conv_ln_b"], lw["pool_w"],
      lw["pool_scale"])


def _swa_heads(q, kk, vv, bias, sink_ref, o_ref):
    bq = q.shape[0]
    lane = lax.broadcasted_iota(jnp.int32, (bq, LANES), 1)
    low = lane < 64
    zero = jnp.zeros((), BF16)
    one = jnp.ones((), BF16)
    dn = (((1,), (1,)), ((), ()))
    lane_k = lax.broadcasted_iota(jnp.int32, (kk.shape[0], LANES), 1) < 64
    for g in range(2):
        pairs = (q[:, LANES * (2 * g):LANES * (2 * g + 1)], q[:, LANES * (2 * g + 1):LANES * (2 * g + 2)])
        qs = jnp.concatenate([jnp.where(low, pairs[0], zero), jnp.where(low, pairs[1], zero),
                              jnp.where(low, zero, pairs[0]), jnp.where(low, zero, pairs[1])], axis=0)
        s = lax.dot_general(qs, kk[:, LANES * g:LANES * (g + 1)], dn, preferred_element_type=F32)
        if bias is not None:
            s = s + jnp.concatenate([bias] * 4, axis=0)
        heads = (4 * g, 4 * g + 2, 4 * g + 1, 4 * g + 3)
        sk = jnp.concatenate([jnp.broadcast_to(sink_ref[h:h + 1, :], (bq, LANES)) for h in heads], axis=0)
        m = jnp.maximum(jnp.max(s, axis=-1, keepdims=True), sk)
        e_sink = jnp.exp2(sk - m)
        p = jnp.concatenate([jnp.exp2(s[:, LANES * t:LANES * (t + 1)] - m).astype(BF16)
                             for t in range(s.shape[1] // LANES)], axis=1)
        vg = vv[:, LANES * g:LANES * (g + 1)]
        o_even = jnp.dot(p[:2 * bq], jnp.where(lane_k, vg, one), preferred_element_type=F32)
        o_odd = jnp.dot(p[2 * bq:], jnp.where(lane_k, one, vg), preferred_element_type=F32)
        out_even = o_even * pltpu.roll(1.0 / (o_even + e_sink[:2 * bq]), 64, 1)
        out_odd = o_odd * pltpu.roll(1.0 / (o_odd + e_sink[2 * bq:]), 64, 1)
        for jj in range(2):
            rows = slice(jj * bq, (jj + 1) * bq)
            o_ref[:, LANES * (2 * g + jj):LANES * (2 * g + jj + 1)] = jnp.where(
                low, out_even[rows], out_odd[rows]).astype(BF16)


def _swa_lat_kernel(q_ref, kp_ref, kc_ref, kn_ref, vp_ref, vc_ref, vn_ref, kx_ref, vx_ref, sink_ref,
                    o_ref, *, seq_len):
    i = pl.program_id(1)
    kk = jnp.concatenate([kp_ref[...], kc_ref[...], kn_ref[...], kx_ref[...]], axis=0)
    vv = jnp.concatenate([vp_ref[...], vc_ref[...], vn_ref[...], vx_ref[...]], axis=0)
    nk = kk.shape[0]
    r = lax.broadcasted_iota(jnp.int32, (SWA_BLOCK, nk), 0)
    s = lax.broadcasted_iota(jnp.int32, (SWA_BLOCK, nk), 1)
    kpos = (i - 1) * SWA_BLOCK + s
    diff = kpos - (i * SWA_BLOCK + r)
    ok = ((kpos >= 0) & (kpos < seq_len) & (jnp.abs(diff) <= SWA_WINDOW)) | (s >= 3 * SWA_BLOCK)
    _swa_heads(q_ref[...], kk, vv, jnp.where(ok, 0.0, MASK_VALUE), sink_ref, o_ref)


def _swa_ctx_kernel(q_ref, kx_ref, vx_ref, sink_ref, o_ref):
    _swa_heads(q_ref[...], kx_ref[...], vx_ref[...], None, sink_ref, o_ref)


def _swa(q, k2, v2, sink_b, geo, with_ctx):
    nb, ls, lc = geo["B"], geo["L"], geo["C"]
    nq = ls // SWA_BLOCK
    ctx_blk0 = nb * ls // lc
    qspec = pl.BlockSpec((SWA_BLOCK, 512), lambda b, i: (b * nq + i, 0))
    kv = lambda off: pl.BlockSpec(
        (SWA_BLOCK, 256), lambda b, i: (b * nq + jnp.clip(i + off, 0, nq - 1), 0))
    cx = pl.BlockSpec((lc, 256), lambda b, i: (ctx_blk0 + b, 0))
    sk = pl.BlockSpec((SUBLANES, LANES), lambda b, i: (0, 0))
    yb = pl.pallas_call(
        functools.partial(_swa_lat_kernel, seq_len=ls),
        grid=(nb, nq),
        in_specs=[qspec, kv(-1), kv(0), kv(1), kv(-1), kv(0), kv(1), cx, cx, sk],
        out_specs=qspec,
        out_shape=jax.ShapeDtypeStruct((nb * ls, 512), BF16),
        compiler_params=_cparams(("parallel", "parallel")),
        name="swa_latent",
    )(q, k2, k2, k2, v2, v2, v2, k2, v2, sink_b)
    if not with_ctx:
        return yb, None
    cx1 = pl.BlockSpec((lc, 256), lambda b: (ctx_blk0 + b, 0))
    yb_ctx = pl.pallas_call(
        _swa_ctx_kernel,
        grid=(nb,),
        in_specs=[pl.BlockSpec((lc, 512), lambda b: (ctx_blk0 + b, 0)), cx1, cx1,
                  pl.BlockSpec((SUBLANES, LANES), lambda b: (0, 0))],
        out_specs=pl.BlockSpec((lc, 512), lambda b: (b, 0)),
        out_shape=jax.ShapeDtypeStruct((nb * lc, 512), BF16),
        compiler_params=_cparams(("parallel",)),
        name="swa_context",
    )(q, k2, v2, sink_b)
    return yb, yb_ctx


MLA_KC = 256
MLA_ROWS = MLA_HEADS * MLA_BLOCK
MLA_SOFTMAX_ROWS = 64


def _mla_kernel(cq_ref, *rest, n_lat, n_ctx):
    if n_lat:
        kl_ref, rest = rest[0], rest[1:]
    (kx_ref, qg_ref, wuq_ref, wukp_ref, wuvp_ref, cs_ref, sn_ref, o_ref,
     q_sc, s_a, s_b, p_a, p_b, al_a, al_b, m_sc, acc_sc) = rest
    kc = MLA_KC
    n = n_lat + n_ctx
    scale = (MLA_NOPE + MLA_ROPE) ** -0.5 * LOG2E
    cq = cq_ref[...].astype(F32)
    cqn = cq * lax.rsqrt(jnp.mean(cq * cq, axis=-1, keepdims=True) + LN_EPS) * qg_ref[...]
    q = jnp.dot(cqn.astype(BF16), wuq_ref[...], preferred_element_type=F32)
    qr = q[:, 512:768]
    lane2 = lax.broadcasted_iota(jnp.int32, (MLA_BLOCK, 2 * LANES), 1)
    rot = jnp.where((lane2 % 32) < 16, pltpu.roll(qr, 2 * LANES - 16, 1), pltpu.roll(qr, 16, 1))
    qr = (qr * cs_ref[...] + rot * sn_ref[...]) * scale
    slot_of_lane = lax.broadcasted_iota(jnp.int32, (MLA_BLOCK, LANES), 1) // 32
    for j in range(MLA_HEADS // 2):
        qa2 = jnp.dot(q[:, LANES * j:LANES * (j + 1)].astype(BF16), wukp_ref[j],
                      preferred_element_type=F32) * scale
        for e in range(2):
            h = 2 * j + e
            rows = slice(h * MLA_BLOCK, (h + 1) * MLA_BLOCK)
            qrh = jnp.where(slot_of_lane == (h % 4), qr[:, LANES * (h // 4):LANES * (h // 4 + 1)], 0.0)
            q_sc[rows, 0:LANES] = qa2[:, LANES * e:LANES * (e + 1)].astype(BF16)
            q_sc[rows, LANES:2 * LANES] = qrh.astype(BF16)

    m_sc[...] = jnp.full(m_sc.shape, MASK_VALUE, F32)
    acc_sc[...] = jnp.zeros(acc_sc.shape, F32)
    dn = (((1,), (1,)), ((), ()))
    s_bufs, p_bufs, al_bufs = (s_a, s_b), (p_a, p_b), (al_a, al_b)
    ones = jnp.ones((kc, LANES), BF16)

    def keys(c):
        if isinstance(c, int):
            if c >= n_lat:
                return kx_ref[(c - n_lat) * kc:(c - n_lat + 1) * kc, :]
            return kl_ref[c * kc:(c + 1) * kc, :]
        return kl_ref[pl.ds(pl.multiple_of(c * kc, kc), kc), :]

    def scores(c, par):
        s_bufs[par][...] = lax.dot_general(q_sc[...], keys(c), dn, preferred_element_type=F32)

    def softmax(par):
        s_ref, p_ref, al_ref = s_bufs[par], p_bufs[par], al_bufs[par]
        for rb in range(MLA_ROWS // MLA_SOFTMAX_ROWS):
            rows = slice(rb * MLA_SOFTMAX_ROWS, (rb + 1) * MLA_SOFTMAX_ROWS)
            sv = s_ref[rows, :]
            m_old = m_sc[rows, :]
            m_new = jnp.maximum(m_old, jnp.max(sv, axis=-1, keepdims=True))
            al_ref[rows, :] = jnp.exp2(m_old - m_new)
            m_sc[rows, :] = m_new
            for t in range(kc // LANES):
                lanes = slice(LANES * t, LANES * (t + 1))
                p_ref[rows, lanes] = jnp.exp2(sv[:, lanes] - m_new).astype(BF16)

    def values(c, par):
        v_aug = jnp.concatenate([keys(c)[:, 0:LANES], ones], axis=1)
        pv = jnp.dot(p_bufs[par][...], v_aug, preferred_element_type=F32)
        al = al_bufs[par][...]
        acc_sc[...] = jnp.concatenate([al, al], axis=1) * acc_sc[...] + pv

    def step(t, par):
        static = isinstance(t, int)
        if not static or t < n:
            scores(t, par)
        if not static or 1 <= t <= n:
            softmax(1 - par)
        if not static or 2 <= t <= n + 1:
            values(t - 2, par)

    loop_lo = 2
    loop_hi = max(loop_lo, n_lat)
    n_pairs = (loop_hi - loop_lo) // 2
    for t in range(0, loop_lo):
        step(t, t % 2)
    if n_pairs:
        def pair(jp, carry):
            t0 = loop_lo + 2 * jp
            step(t0, 0)
            step(t0 + 1, 1)
            return carry
        lax.fori_loop(0, n_pairs, pair, 0)
    for t in range(loop_lo + 2 * n_pairs, n + 2):
        step(t, t % 2)

    acc = acc_sc[...]
    o = acc[:, 0:LANES] / acc[:, LANES:2 * LANES]
    for j in range(MLA_HEADS // 2):
        o2 = jnp.concatenate([o[(2 * j) * MLA_BLOCK:(2 * j + 1) * MLA_BLOCK, :],
                              o[(2 * j + 1) * MLA_BLOCK:(2 * j + 2) * MLA_BLOCK, :]], axis=1)
        o_ref[:, LANES * j:LANES * (j + 1)] = jnp.dot(
            o2.astype(BF16), wuvp_ref[j], preferred_element_type=F32).astype(BF16)


def _mla(cq, kc_all, lw, tabs, geo, lat):
    nb, ls, lc = geo["B"], geo["L"], geo["C"]
    ctx_blk0 = nb * ls // lc
    if lat:
        nq = ls // MLA_BLOCK
        qmap = lambda b, i: (b * nq + i, 0)
        tmap = lambda b, i: (i, 0)
        n_lat = ls // MLA_KC
    else:
        nq = lc // MLA_BLOCK
        qmap = lambda b, i: (nb * (ls // MLA_BLOCK) + b * nq + i, 0)
        tmap = lambda b, i: (ls // MLA_BLOCK, 0)
        n_lat = 0
    omap = lambda b, i: (b * nq + i, 0)
    full = lambda shape: pl.BlockSpec(shape, lambda b, i: tuple(0 for _ in shape))
    in_specs = [pl.BlockSpec((MLA_BLOCK, 256), qmap)]
    args = [cq]
    if lat:
        in_specs.append(pl.BlockSpec((ls, 256), lambda b, i: (b, 0)))
        args.append(kc_all)
    in_specs += [pl.BlockSpec((lc, 256), lambda b, i: (ctx_blk0 + b, 0)),
                 full((1, 256)), full((256, 768)), full((4, LANES, 256)), full((4, 256, LANES)),
                 pl.BlockSpec((MLA_BLOCK, 256), tmap), pl.BlockSpec((MLA_BLOCK, 256), tmap)]
    args += [kc_all, lw["mla_q_g"], lw["wuq"], lw["wukp"], lw["wuvp"], tabs["cq"], tabs["sq"]]
    return pl.pallas_call(
        functools.partial(_mla_kernel, n_lat=n_lat, n_ctx=lc // MLA_KC),
        grid=(nb, nq),
        in_specs=in_specs,
        out_specs=pl.BlockSpec((MLA_BLOCK, 512), omap),
        out_shape=jax.ShapeDtypeStruct((nb * nq * MLA_BLOCK, 512), BF16),
        scratch_shapes=[pltpu.VMEM((MLA_ROWS, 256), BF16),
                        pltpu.VMEM((MLA_ROWS, MLA_KC), F32), pltpu.VMEM((MLA_ROWS, MLA_KC), F32),
                        pltpu.VMEM((MLA_ROWS, MLA_KC), BF16), pltpu.VMEM((MLA_ROWS, MLA_KC), BF16),
                        pltpu.VMEM((MLA_ROWS, LANES), F32), pltpu.VMEM((MLA_ROWS, LANES), F32),
                        pltpu.VMEM((MLA_ROWS, LANES), F32), pltpu.VMEM((MLA_ROWS, 2 * LANES), F32)],
        compiler_params=_cparams(("parallel", "parallel")),
        name="mla_latent" if lat else "mla_context",
    )(*args)


def _merge_kernel(x_ref, *rest, alpha, n_lat_tiles, has_ctx):
    n_y = 2 * N_BRANCH if has_ctx else N_BRANCH
    y_refs, rest = rest[:n_y], rest[n_y:]
    (sh1_ref, sc1_ref, g1_ref, sh2_ref, sc2_ref, wg_ref, bg_ref, wb_ref, wo_ref, l1g_ref, l1b_ref,
     rw_ref, rb_ref, x1_ref, h2_ref, ei_ref, ew_ref) = rest
    tm = x_ref.shape[0]
    is_ctx = pl.program_id(0) >= n_lat_tiles
    x = x_ref[...]
    h = (_ln(x) * (1.0 + sc1_ref[...]) + sh1_ref[...]).astype(BF16)
    merged = None
    for i in range(N_BRANCH):
        yi = y_refs[i][...]
        if has_ctx:
            yi = jnp.where(is_ctx, y_refs[N_BRANCH + i][...], yi)
        gate = _sigmoid(jnp.dot(h, wg_ref[i], preferred_element_type=F32) + bg_ref[i])
        term = gate * jnp.dot(yi, wb_ref[i], preferred_element_type=F32)
        merged = term if merged is None else merged + term
    y = jnp.dot(merged.astype(BF16), wo_ref[...], preferred_element_type=F32)
    x1 = _ln(alpha * x + g1_ref[...] * y) * l1g_ref[...] + l1b_ref[...]
    x1_ref[...] = x1
    h2 = _ln(x1) * (1.0 + sc2_ref[...]) + sh2_ref[...]
    for j in range(ROW_TILES):
        h2_ref[pl.ds(j, tm, stride=ROW_TILES), :] = h2[:, LANES * j:LANES * (j + 1)]
    h2_hi = h2.astype(BF16)
    h2_lo = (h2 - h2_hi.astype(F32)).astype(BF16)
    t = jnp.dot(h2_hi, rw_ref[...], preferred_element_type=F32)
    lg = (t[:, 0:LANES] + t[:, LANES:2 * LANES]
          + jnp.dot(h2_lo, rw_ref[:, 0:LANES], preferred_element_type=F32) + rb_ref[...])
    lane = lax.broadcasted_iota(jnp.int32, (tm, LANES), 1).astype(F32)
    vals, idxs = [], []
    for _ in range(TOP_K):
        mx = jnp.max(lg, axis=-1, keepdims=True)
        ix = jnp.min(jnp.where(lg == mx, lane, float(LANES)), axis=-1, keepdims=True)
        vals.append(mx)
        idxs.append(ix)
        lg = jnp.where(lane == ix, -3.0e38, lg)
    exps = [jnp.exp(v - vals[0]) for v in vals]
    den = exps[0] + exps[1] + exps[2] + exps[3]
    ei = jnp.zeros((tm, LANES), F32)
    ew = jnp.zeros((tm, LANES), F32)
    for k in range(TOP_K):
        ei = jnp.where(lane == float(k), idxs[k], ei)
        ew = jnp.where(lane == float(k), exps[k] / den, ew)
    ei_ref[...] = ei.astype(jnp.int32)
    ew_ref[...] = ew


def _merge(x, ys_lat, ys_ctx, mods, lw, geo, alpha):
    tm, n_lat_tiles, tps, nb = geo["tm"], geo["n_lat_tiles"], geo["tps"], geo["B"]
    d = x.shape[1]
    has_ctx = ys_ctx is not None
    n_tiles = geo["n_tiles"] if has_ctx else n_lat_tiles
    n_ctx_tiles = geo["n_tiles"] - n_lat_tiles
    t_rows = n_tiles * tm

    def cond(i):
        return jnp.where(i < n_lat_tiles, i // tps, nb)

    row = lambda w: pl.BlockSpec((tm, w), lambda i: (i, 0))
    y_lat = pl.BlockSpec((tm, BRANCH_W), lambda i: (jnp.minimum(i, n_lat_tiles - 1), 0))
    y_ctx = pl.BlockSpec((tm, BRANCH_W), lambda i: (jnp.clip(i - n_lat_tiles, 0, n_ctx_tiles - 1), 0))
    y_specs = [y_lat] * N_BRANCH + ([y_ctx] * N_BRANCH if has_ctx else [])
    ys = tuple(ys_lat) + (tuple(ys_ctx) if has_ctx else ())
    mod = pl.BlockSpec((None, 1, d), lambda i: (cond(i), 0, 0))
    full = lambda shape: pl.BlockSpec(shape, lambda i: tuple(0 for _ in shape), pipeline_mode=pl.Buffered(1))
    return pl.pallas_call(
        functools.partial(_merge_kernel, alpha=alpha, n_lat_tiles=n_lat_tiles, has_ctx=has_ctx),
        grid=(n_tiles,),
        name="merge_router",
        in_specs=[row(d)] + y_specs + [mod, mod, mod, mod, mod,
                  full((4, d, d)), full((4, 1, d)), full((4, BRANCH_W, d)), full((d, d)),
                  full((1, d)), full((1, d)), full((d, 2 * LANES)), full((1, LANES))],
        out_specs=[row(d), pl.BlockSpec((tm * ROW_TILES, LANES), lambda i: (i, 0)), row(LANES), row(LANES)],
        out_shape=[jax.ShapeDtypeStruct((t_rows, d), F32),
                   jax.ShapeDtypeStruct((t_rows * ROW_TILES, LANES), F32),
                   jax.ShapeDtypeStruct((t_rows, LANES), jnp.int32),
                   jax.ShapeDtypeStruct((t_rows, LANES), F32)],
        compiler_params=_cparams(("parallel",)),
    )(x, *ys, mods["sh1"], mods["sc1"], mods["g1"], mods["sh2"], mods["sc2"],
      lw["w_gate"], lw["b_gate"], lw["w_branch"], lw["w_out"], lw["ln1_g"], lw["ln1_b"],
      lw["router_w"], lw["router_b"])


GMM_IDX_SLOTS = 8
GMM_X_SLOTS = 3


def _gmm_kernel(blk_e_ref, nvalid_ref, idx_hbm, h2_hbm, wgu_ref, bgu_ref, wdn_ref, bdn_ref,
                ytk_hbm, idx_sm, xbuf, ybuf, sem_i, sem_g, sem_s):
    del blk_e_ref
    b = pl.program_id(0)
    n_blocks = pl.num_programs(0)
    nvalid = nvalid_ref[0]
    bm = xbuf.shape[1] // ROW_TILES
    rows = bm * ROW_TILES
    pad0 = ytk_hbm.shape[0] - rows

    def idx_copy(blk):
        slot = blk % GMM_IDX_SLOTS
        return pltpu.make_async_copy(idx_hbm.at[jnp.minimum(blk, n_blocks - 1)], idx_sm.at[slot], sem_i.at[slot])

    def gather_row(blk, r):
        src = pl.multiple_of(idx_sm[blk % GMM_IDX_SLOTS, 0, r] * ROW_TILES, ROW_TILES)
        slot = blk % GMM_X_SLOTS
        pltpu.make_async_copy(h2_hbm.at[pl.ds(src, ROW_TILES), :],
                              xbuf.at[slot, pl.ds(r * ROW_TILES, ROW_TILES), :], sem_g.at[slot]).start()

    def scatter_row(blk, r):
        dst = pl.multiple_of(idx_sm[(blk + GMM_IDX_SLOTS) % GMM_IDX_SLOTS, 1, r] * ROW_TILES, ROW_TILES)
        slot = (blk + 2) % 2
        pltpu.make_async_copy(ybuf.at[slot, pl.ds(r * ROW_TILES, ROW_TILES), :],
                              ytk_hbm.at[pl.ds(dst, ROW_TILES), :], sem_s.at[slot]).start()

    def gather_wait(blk):
        slot = blk % GMM_X_SLOTS
        pltpu.make_async_copy(h2_hbm.at[pl.ds(0, rows), :], xbuf.at[slot], sem_g.at[slot]).wait()

    def scatter_wait(blk):
        slot = (blk + 2) % 2
        pltpu.make_async_copy(ybuf.at[slot], ytk_hbm.at[pl.ds(0, rows), :], sem_s.at[slot]).wait()

    def rolled(fn):
        def body(r, carry):
            fn(r)
            return carry
        lax.fori_loop(0, bm, body, 0)

    @pl.when(b == 0)
    def _():
        ybuf[1] = jnp.zeros(ybuf.shape[1:], F32)

        def fill(r):
            idx_sm[GMM_IDX_SLOTS - 1, 1, r] = pad0 // ROW_TILES + r

        rolled(fill)
        for blk in (0, 1):
            idx_copy(blk).start()
            idx_copy(blk).wait()
        idx_copy(2).start()
        rolled(lambda r: gather_row(0, r))
        rolled(lambda r: gather_row(1, r))

    @pl.when(b < nvalid)
    def _():
        par = b % 2
        idx_copy(b + 3).start()
        idx_copy(b + 2).wait()
        gather_wait(b)
        xs = jnp.concatenate(
            [xbuf[b % GMM_X_SLOTS, pl.ds(j, bm, stride=ROW_TILES), :] for j in range(ROW_TILES)],
            axis=1).astype(BF16)
        for r in range(bm):
            gather_row(b + 2, r)
        for r in range(bm):
            scatter_row(b - 1, r)
        z = jnp.dot(xs, wgu_ref[...], preferred_element_type=F32) + bgu_ref[...]
        gate = jnp.minimum(z[:, :D_EXPERT], SWIGLU_LIMIT)
        up = jnp.clip(z[:, D_EXPERT:], -SWIGLU_LIMIT, SWIGLU_LIMIT)
        act = (up + 1.0) * gate * _sigmoid(SWIGLU_ALPHA * gate)
        y = jnp.dot(act.astype(BF16), wdn_ref[...], preferred_element_type=F32) + bdn_ref[...]
        for j in range(ROW_TILES):
            ybuf[par, pl.ds(j, bm, stride=ROW_TILES), :] = y[:, LANES * j:LANES * (j + 1)]
        scatter_wait(b - 1)

    @pl.when(b == nvalid - 1)
    def _():
        rolled(lambda r: scatter_row(b, r))
        scatter_wait(b)
        gather_wait(b + 1)
        gather_wait(b + 2)
        idx_copy(b + 3).wait()


def _gmm(h2g, plan, lw, n_tok):
    bm = EXPERT_BM
    n_blocks = plan["idx"].shape[0]
    d = lw["w_gu"].shape[1]
    n_slots = n_tok * TOP_K + bm
    grid_spec = pltpu.PrefetchScalarGridSpec(
        num_scalar_prefetch=2,
        grid=(n_blocks,),
        in_specs=[
            pl.BlockSpec(memory_space=pl.ANY),
            pl.BlockSpec(memory_space=pl.ANY),
            pl.BlockSpec((None, d, 2 * D_EXPERT), lambda b, be, nv: (be[b], 0, 0)),
            pl.BlockSpec((None, 1, 2 * D_EXPERT), lambda b, be, nv: (be[b], 0, 0)),
            pl.BlockSpec((None, D_EXPERT, d), lambda b, be, nv: (be[b], 0, 0)),
            pl.BlockSpec((None, 1, d), lambda b, be, nv: (be[b], 0, 0)),
        ],
        out_specs=pl.BlockSpec(memory_space=pl.ANY),
        scratch_shapes=[pltpu.SMEM((GMM_IDX_SLOTS, 2, bm), jnp.int32),
                        pltpu.VMEM((GMM_X_SLOTS, bm * ROW_TILES, LANES), F32),
                        pltpu.VMEM((2, bm * ROW_TILES, LANES), F32),
                        pltpu.SemaphoreType.DMA((GMM_IDX_SLOTS,)),
                        pltpu.SemaphoreType.DMA((GMM_X_SLOTS,)),
                        pltpu.SemaphoreType.DMA((2,))],
    )
    return pl.pallas_call(
        _gmm_kernel,
        grid_spec=grid_spec,
        out_shape=jax.ShapeDtypeStruct((n_slots * ROW_TILES, LANES), F32),
        compiler_params=_cparams(("arbitrary",)),
        name="expert_gmm",
    )(plan["blk_e"], plan["nvalid"], plan["idx"], h2g, lw["w_gu"], lw["b_gu"], lw["w_down"], lw["b_down"])


def _plan(ei, n_tok):
    bm = EXPERT_BM
    n_assign = n_tok * TOP_K
    flat_e = ei.reshape(-1)
    order = jnp.argsort(flat_e, stable=True).astype(jnp.int32)
    experts = jnp.arange(N_EXPERTS, dtype=jnp.int32)
    counts = jnp.sum((flat_e[:, None] == experts[None, :]).astype(jnp.int32), axis=0)
    padded = (counts + bm - 1) // bm * bm
    pad_end = jnp.cumsum(padded)
    pad_start = pad_end - padded
    grp_start = jnp.cumsum(counts) - counts
    n_blocks = -(-(n_assign + N_EXPERTS * (bm - 1)) // bm)
    blk_first = jnp.arange(n_blocks, dtype=jnp.int32) * bm
    blk_e = jnp.minimum(jnp.sum((pad_end[None, :] <= blk_first[:, None]).astype(jnp.int32), axis=1),
                        N_EXPERTS - 1)
    within = blk_first[:, None] + jnp.arange(bm, dtype=jnp.int32)[None, :] - pad_start[blk_e][:, None]
    valid = (within < counts[blk_e][:, None]) & (blk_first < pad_end[-1])[:, None]
    pos = jnp.clip(grp_start[blk_e][:, None] + within, 0, n_assign - 1)
    row_asg = order[pos]
    row_src = jnp.where(valid, row_asg // TOP_K, 0)
    row_dst = jnp.where(valid, (row_asg % TOP_K) * n_tok + row_asg // TOP_K,
                        n_assign + jnp.arange(bm, dtype=jnp.int32)[None, :])
    idx = jnp.stack([row_src, row_dst], axis=1).astype(jnp.int32)
    nvalid = (pad_end[-1:] // bm).astype(jnp.int32)
    return {"idx": idx, "blk_e": blk_e.astype(jnp.int32), "nvalid": nvalid}


def _combine_kernel(x_ref, y0_ref, y1_ref, y2_ref, y3_ref, ew_ref, g2_ref, lg_ref, lb_ref, o_ref, *, alpha):
    tm = x_ref.shape[0]
    ew = ew_ref[...]
    cols = []
    for j in range(ROW_TILES):
        acc = None
        for k, y_ref in enumerate((y0_ref, y1_ref, y2_ref, y3_ref)):
            piece = y_ref[pl.ds(j, tm, stride=ROW_TILES), :] * ew[:, k:k + 1]
            acc = piece if acc is None else acc + piece
        cols.append(acc)
    m = jnp.concatenate(cols, axis=1)
    o_ref[...] = _ln(alpha * x_ref[...] + g2_ref[...] * m) * lg_ref[...] + lb_ref[...]


def _combine(x1, ytk, ew, g2, lw, geo, n_tiles, alpha):
    tm, n_lat_tiles, tps, nb = geo["tmc"], geo["n_lat_tiles"], geo["tps"], geo["B"]
    ratio = geo["tm"] // tm
    d = x1.shape[1]

    def cond(i):
        return jnp.where(i < n_lat_tiles * ratio, i // (tps * ratio), nb)

    def slot_map(k, i):
        return (k * n_tiles * ratio + i, 0)

    row = lambda w: pl.BlockSpec((tm, w), lambda i: (i, 0))
    full = lambda shape: pl.BlockSpec(shape, lambda i: tuple(0 for _ in shape))
    return pl.pallas_call(
        functools.partial(_combine_kernel, alpha=alpha),
        grid=(n_tiles * ratio,),
        in_specs=[row(d)] + [pl.BlockSpec((tm * ROW_TILES, LANES), functools.partial(slot_map, k))
                             for k in range(TOP_K)] + [row(LANES),
                  pl.BlockSpec((None, 1, d), lambda i: (cond(i), 0, 0)), full((1, d)), full((1, d))],
        out_specs=row(d),
        out_shape=jax.ShapeDtypeStruct((n_tiles * tm * ratio, d), F32),
        compiler_params=_cparams(("parallel",)),
        name="moe_combine",
    )(x1, ytk, ytk, ytk, ytk, ew, g2, lw["ln2_g"], lw["ln2_b"])


def _rope_tables(seq_len, tm):
    rows = seq_len // GRID_W
    row = jnp.repeat(jnp.arange(rows, dtype=F32), GRID_W)
    col = jnp.tile(jnp.arange(GRID_W, dtype=F32), rows)

    def table(rot_dim, reps, pad_rows):
        n_freq = rot_dim // 4
        inv = ROPE_THETA ** (-jnp.arange(n_freq, dtype=F32) / n_freq)
        ang = jnp.concatenate([row[:, None] * inv, col[:, None] * inv], axis=-1)
        cos, sin = jnp.cos(ang), jnp.sin(ang)
        c = jnp.tile(jnp.concatenate([cos, cos], axis=-1), (1, reps))
        s = jnp.tile(jnp.concatenate([-sin, sin], axis=-1), (1, reps))
        c = jnp.concatenate([c, jnp.ones((pad_rows, c.shape[1]), F32)], axis=0)
        s = jnp.concatenate([s, jnp.zeros((pad_rows, s.shape[1]), F32)], axis=0)
        return c, s

    cs, sn = table(HEAD_DIM, 2, tm)
    ck, sk = table(MLA_ROPE, 4, tm)
    cq, sq = table(MLA_ROPE, 8, MLA_BLOCK)
    return {"cs": cs, "sn": sn, "ck": ck, "sk": sk, "cq": cq, "sq": sq}


def _layer_weights(l, w_in, conv_w, conv_b, conv_ln_g, conv_ln_b, swa_sink, pool_w, pool_scale, mla_q_g,
                   mla_w_uq, mla_kv_g, mla_w_uk, mla_w_uv, w_branch, w_gate, b_gate, w_out, ln1_g, ln1_b,
                   router_w, router_b, w_gu_bf, b_gu, w_down_bf, b_down, ln2_g, ln2_b):
    d = w_in.shape[1]
    wi = w_in[l]
    kr = wi[:, 2688:2720]
    w_all = jnp.concatenate([wi[:, :2688], kr, kr, kr, kr], axis=1).astype(BF16)
    wuq = mla_w_uq[l].reshape(MLA_Q_RANK, MLA_HEADS, MLA_NOPE + MLA_ROPE)
    wuq = jnp.concatenate([wuq[:, :, :MLA_NOPE].reshape(MLA_Q_RANK, -1),
                           wuq[:, :, MLA_NOPE:].reshape(MLA_Q_RANK, -1)], axis=1).astype(BF16)
    wuk = jnp.transpose(mla_w_uk[l], (1, 2, 0))
    wuv = jnp.transpose(mla_w_uv[l], (1, 0, 2))
    zk = jnp.zeros((MLA_NOPE, MLA_KV_RANK), F32)
    zv = jnp.zeros((MLA_KV_RANK, MLA_V), F32)
    wukp = jnp.stack([jnp.block([[wuk[2 * j], zk], [zk, wuk[2 * j + 1]]]) for j in range(MLA_HEADS // 2)])
    wuvp = jnp.stack([jnp.block([[wuv[2 * j], zv], [zv, wuv[2 * j + 1]]]) for j in range(MLA_HEADS // 2)])
    win_i = jnp.arange(CONV_ROWS + 2 * PAD_ROWS)
    rw = jnp.zeros((d, LANES), F32).at[:, :N_EXPERTS].set(router_w[l])
    rw_hi = rw.astype(BF16)
    rb = jnp.full((1, LANES), MASK_VALUE, F32).at[0, :N_EXPERTS].set(router_b[l])
    return {
        "w_all": w_all,
        "conv_w": jnp.repeat(jnp.concatenate([conv_w[l], jnp.zeros((1, CONV_CH), F32)], axis=0), SUBLANES, axis=0),
        "conv_shift": (win_i[None, None, :] == win_i[None, :, None] + jnp.arange(SUBLANES)[:, None, None]).astype(BF16),
        "conv_b": conv_b[l][None], "conv_ln_g": conv_ln_g[l][None], "conv_ln_b": conv_ln_b[l][None],
        "sink": jnp.broadcast_to(swa_sink[l][:, None] * LOG2E, (SWA_HEADS, LANES)),
        "pool_w": pool_w[l].astype(BF16), "pool_scale": pool_scale[l][None],
        "mla_q_g": mla_q_g[l][None], "mla_kv_g": mla_kv_g[l][None],
        "wuq": wuq, "wukp": wukp.astype(BF16), "wuvp": wuvp.astype(BF16),
        "w_branch": w_branch[l].astype(BF16), "w_gate": w_gate[l].astype(BF16),
        "b_gate": b_gate[l][:, None, :], "w_out": w_out[l].astype(BF16),
        "ln1_g": ln1_g[l][None], "ln1_b": ln1_b[l][None],
        "router_w": jnp.concatenate([rw_hi, (rw - rw_hi.astype(F32)).astype(BF16)], axis=1), "router_b": rb,
        "w_gu": w_gu_bf[l], "b_gu": b_gu[l][:, None, :],
        "w_down": w_down_bf[l], "b_down": b_down[l][:, None, :],
        "ln2_g": ln2_g[l][None], "ln2_b": ln2_b[l][None],
    }


def _token_tile(seq_len, ctx_rows):
    for tm in (512, 256, 128):
        if seq_len % tm == 0 and ctx_rows % tm == 0:
            return tm
    raise ValueError("unsupported sequence / context lengths")


def kernel(x, c, ctx, c_ctx, w_ada, b_ada, w_in, conv_w, conv_b, conv_ln_g, conv_ln_b, swa_sink, pool_w,
           pool_scale, mla_q_g, mla_w_uq, mla_kv_g, mla_w_uk, mla_w_uv, w_branch, w_gate, b_gate, w_out,
           ln1_g, ln1_b, router_w, router_b, w_gu, b_gu, w_down, b_down, ln2_g, ln2_b):
    nb, ls, d = x.shape
    lc = ctx.shape[1]
    depth = w_ada.shape[0]
    assert ls % GRID_W == 0 and ls % SWA_BLOCK == 0 and (nb * ls) % lc == 0
    assert ls % MLA_KC == 0 and lc % MLA_KC == 0 and lc % MLA_BLOCK == 0
    tm = _token_tile(ls, nb * lc)
    t_lat, t_ctx = nb * ls, nb * lc
    geo = {"B": nb, "L": ls, "C": lc, "tm": tm, "tmc": min(tm, 256), "tps": ls // tm,
           "n_lat_tiles": t_lat // tm, "n_tiles": (t_lat + t_ctx) // tm}
    alpha = (2 * depth) ** 0.25

    ra = -(-(nb + 1) // SUBLANES) * SUBLANES
    cvec = jnp.zeros((ra, d), F32).at[:nb].set(c).at[nb].set(c_ctx)
    ada = _ada_terms(cvec, w_ada, b_ada).reshape(depth, ra, 6, 1, d)
    tabs = _rope_tables(ls, tm)
    w_gu_bf = w_gu.astype(BF16)
    w_down_bf = w_down.astype(BF16)

    xa = jnp.concatenate([x.reshape(t_lat, d), ctx.reshape(t_ctx, d)], axis=0)
    for l in range(depth):
        ctx_out = l < depth - 1
        n_tiles = geo["n_tiles"] if ctx_out else geo["n_lat_tiles"]
        names = ("sh1", "sc1", "g1", "sh2", "sc2", "g2")
        mods = {n: ada[l, :, i] for i, n in enumerate(names)}
        lw = _layer_weights(l, w_in, conv_w, conv_b, conv_ln_g, conv_ln_b, swa_sink, pool_w, pool_scale,
                            mla_q_g, mla_w_uq, mla_kv_g, mla_w_uk, mla_w_uv, w_branch, w_gate, b_gate,
                            w_out, ln1_g, ln1_b, router_w, router_b, w_gu_bf, b_gu, w_down_bf, b_down,
                            ln2_g, ln2_b)
        a, q, k2, v2, p, cq, kc = _in_proj(xa, mods["sh1"], mods["sc1"], lw["w_all"], tabs,
                                           lw["mla_kv_g"], geo)
        ya, yc = _local_mixers(a, p, lw, ls, 0, nb)
        yb, yb_ctx = _swa(q, k2, v2, lw["sink"], geo, ctx_out)
        yd = _mla(cq, kc, lw, tabs, geo, True)
        ys_ctx = None
        if ctx_out:
            ya_ctx, yc_ctx = _local_mixers(a, p, lw, lc, t_lat // lc, nb)
            yd_ctx = _mla(cq, kc, lw, tabs, geo, False)
            ys_ctx = (ya_ctx, yb_ctx, yc_ctx, yd_ctx)
        x1, h2g, ei, ew = _merge(xa, (ya, yb, yc, yd), ys_ctx, mods, lw, geo, alpha)
        n_tok = n_tiles * tm
        plan = _plan(ei[:, :TOP_K], n_tok)
        ytk = _gmm(h2g, plan, lw, n_tok)
        xa = _combine(x1, ytk, ew, mods["g2"], lw, geo, n_tiles, alpha)
    return xa[:t_lat].reshape(nb, ls, d)
```

```python
import functools

import jax
import jax.numpy as jnp
import numpy as np
from jax import lax
from jax.experimental import pallas as pl
from jax.experimental.pallas import tpu as pltpu

F32 = jnp.float32
BF16 = jnp.bfloat16
HIGHEST = lax.Precision.HIGHEST

GRID_W = 64
HEAD_DIM = 64
ROPE_THETA = 10000.0
LN_EPS = 1e-6
MASK_VALUE = -1e30
CONV_CH = 512
CONV_K = 31
SWA_HEADS = 8
SWA_WINDOW = 128
SWA_BLOCK = 128
POOL_CH = 512
POOL_WINDOWS = (2, 4, 8, 16)
POOL_GROUP = 128
MLA_HEADS = 8
MLA_Q_RANK = 256
MLA_KV_RANK = 128
MLA_NOPE = 64
MLA_ROPE = 32
MLA_V = 64
MLA_BLOCK = 128
N_BRANCH = 4
BRANCH_W = 512
N_EXPERTS = 32
TOP_K = 4
D_EXPERT = 1024
SWIGLU_LIMIT = 7.0
SWIGLU_ALPHA = 1.702

LOG2E = 1.4426950408889634

LANES = 128
SUBLANES = 8
ROW_TILES = 8
EXPERT_BM = 512
VMEM_LIMIT = 56 * 1024 * 1024


def _cparams(sem):
    return pltpu.CompilerParams(dimension_semantics=sem, vmem_limit_bytes=VMEM_LIMIT)


def _ln(x):
    mu = jnp.mean(x, axis=-1, keepdims=True)
    xc = x - mu
    var = jnp.mean(xc * xc, axis=-1, keepdims=True)
    return xc * lax.rsqrt(var + LN_EPS)


def _sigmoid(x):
    return 0.5 * jnp.tanh(0.5 * x) + 0.5


def _ada_kernel(c_ref, w_ref, b_ref, o_ref):
    c = c_ref[...]
    s = c * _sigmoid(c)
    o_ref[...] = jnp.dot(s, w_ref[...], precision=HIGHEST, preferred_element_type=F32) + b_ref[...]


def _ada_terms(cvec, w_ada, b_ada):
    depth, d, n = w_ada.shape
    ra = cvec.shape[0]
    tn = 1536
    return pl.pallas_call(
        _ada_kernel,
        grid=(depth, n // tn),
        in_specs=[
            pl.BlockSpec((ra, d), lambda l, j: (0, 0)),
            pl.BlockSpec((None, d, tn), lambda l, j: (l, 0, j)),
            pl.BlockSpec((None, 1, tn), lambda l, j: (l, 0, j)),
        ],
        out_specs=pl.BlockSpec((None, ra, tn), lambda l, j: (l, 0, j)),
        out_shape=jax.ShapeDtypeStruct((depth, ra, n), F32),
        compiler_params=_cparams(("parallel", "parallel")),
        name="ada_terms",
    )(cvec, w_ada, b_ada.reshape(depth, 1, n))


IN_COLS = 1024 + 512 + 128 + 128 + 512 + 256 + 128 + 128


def _in_kernel(x_ref, sh_ref, sc_ref, w_ref, cs_ref, sn_ref, ck_ref, sk_ref, kvg_ref,
               a_ref, q_ref, k2_ref, v2_ref, p_ref, cq_ref, kc_ref):
    tm = x_ref.shape[0]
    h = _ln(x_ref[...]) * (1.0 + sc_ref[...]) + sh_ref[...]
    z = jnp.dot(h.astype(BF16), w_ref[...], preferred_element_type=F32)
    a_ref[...] = z[:, 0:1024].astype(BF16)
    cs, sn = cs_ref[...], sn_ref[...]
    lane = lax.broadcasted_iota(jnp.int32, (tm, LANES), 1)
    first32 = (lane % 64) < 32

    def rope64(t):
        rot = jnp.where(first32, pltpu.roll(t, LANES - 32, 1), pltpu.roll(t, 32, 1))
        return t * cs + rot * sn

    for j in range(4):
        qj = rope64(z[:, 1024 + LANES * j:1024 + LANES * (j + 1)]) * (HEAD_DIM ** -0.5 * LOG2E)
        q_ref[:, LANES * j:LANES * (j + 1)] = qj.astype(BF16)
    low64 = lane < 64
    k = rope64(z[:, 1536:1664])
    ks = pltpu.roll(k, 64, 1)
    k2_ref[:, 0:LANES] = jnp.where(low64, k, ks).astype(BF16)
    k2_ref[:, LANES:2 * LANES] = jnp.where(low64, ks, k).astype(BF16)
    v = z[:, 1664:1792]
    vs = pltpu.roll(v, 64, 1)
    v2_ref[:, 0:LANES] = jnp.where(low64, v, vs).astype(BF16)
    v2_ref[:, LANES:2 * LANES] = jnp.where(low64, vs, v).astype(BF16)
    p_ref[...] = z[:, 1792:2304].astype(BF16)
    cq_ref[...] = z[:, 2304:2560].astype(BF16)
    ckv = z[:, 2560:2688]
    ckvn = ckv * lax.rsqrt(jnp.mean(ckv * ckv, axis=-1, keepdims=True) + LN_EPS) * kvg_ref[...]
    kr4 = z[:, 2688:2816]
    first16 = (lane % 32) < 16
    rot = jnp.where(first16, pltpu.roll(kr4, LANES - 16, 1), pltpu.roll(kr4, 16, 1))
    kr4 = kr4 * ck_ref[...] + rot * sk_ref[...]
    kc_ref[:, 0:LANES] = ckvn.astype(BF16)
    kc_ref[:, LANES:2 * LANES] = kr4.astype(BF16)


def _in_proj(x, sh, sc, w_all, tabs, kvg, geo):
    tm, n_tiles, n_lat_tiles, tps, nb = geo["tm"], geo["n_tiles"], geo["n_lat_tiles"], geo["tps"], geo["B"]
    t_rows, d = n_tiles * tm, x.shape[1]

    def cond(i):
        return jnp.where(i < n_lat_tiles, i // tps, nb)

    def pos(i):
        return jnp.where(i < n_lat_tiles, i % tps, tps)

    row = lambda w: pl.BlockSpec((tm, w), lambda i: (i, 0))
    mod = pl.BlockSpec((None, 1, d), lambda i: (cond(i), 0, 0))
    tab = pl.BlockSpec((tm, LANES), lambda i: (pos(i), 0))
    widths = (1024, 512, 256, 256, 512, 256, 256)
    return pl.pallas_call(
        _in_kernel,
        grid=(n_tiles,),
        in_specs=[row(d), mod, mod,
                  pl.BlockSpec((d, IN_COLS), lambda i: (0, 0)),
                  tab, tab, tab, tab,
                  pl.BlockSpec((1, LANES), lambda i: (0, 0))],
        out_specs=[row(w) for w in widths],
        out_shape=[jax.ShapeDtypeStruct((t_rows, w), BF16) for w in widths],
        compiler_params=_cparams(("parallel",)),
        name="in_proj",
    )(x, sh, sc, w_all, tabs["cs"], tabs["sn"], tabs["ck"], tabs["sk"], kvg)


CONV_ROWS = 32
CONV_UNROLL = 4
PAD_ROWS = 16


def _local_kernel(a_ref, p_ref, cw_ref, sh_ref, cb_ref, lg_ref, lb_ref, pw_ref, ps_ref, ya_ref, yc_ref, vpad, upad,
                  *, ls):
    ch = min(256, ls)
    zeros = jnp.zeros((PAD_ROWS, CONV_CH), F32)
    vpad[0:PAD_ROWS, :] = zeros
    vpad[ls + PAD_ROWS:ls + 2 * PAD_ROWS, :] = zeros
    upad[0:PAD_ROWS, :] = zeros
    upad[ls + PAD_ROWS:ls + 2 * PAD_ROWS, :] = zeros

    def fill(c, carry):
        r0 = pl.multiple_of(c * ch, ch)
        a = a_ref[pl.ds(r0, ch), :].astype(F32)
        vpad[pl.ds(r0 + PAD_ROWS, ch), :] = a[:, :CONV_CH] * _sigmoid(a[:, CONV_CH:])
        upad[pl.ds(r0 + PAD_ROWS, ch), :] = p_ref[pl.ds(r0, ch), :].astype(F32)
        return carry

    lax.fori_loop(0, ls // ch, fill, 0)

    cb, lg, lb = cb_ref[...], lg_ref[...], lb_ref[...]
    groups = CONV_ROWS // SUBLANES

    def conv_chunk(r0):
        win = vpad[pl.ds(r0, CONV_ROWS + 2 * PAD_ROWS), :].astype(BF16)
        accs = [jnp.zeros((SUBLANES, CONV_CH), F32) for _ in range(groups)]
        for r in range(SUBLANES):
            wr = win.astype(F32) if r == 0 else jnp.dot(sh_ref[r], win, preferred_element_type=F32)
            for a in range(4):
                k = SUBLANES * a + r - 1
                if 0 <= k < CONV_K:
                    w8 = cw_ref[SUBLANES * k:SUBLANES * (k + 1), :]
                    for g in range(groups):
                        lo = SUBLANES * (a + g)
                        accs[g] = accs[g] + wr[lo:lo + SUBLANES, :] * w8
        y = _ln(jnp.concatenate(accs, axis=0) + cb) * lg + lb
        ya_ref[pl.ds(r0, CONV_ROWS), :] = (y * _sigmoid(y)).astype(BF16)

    def conv(c, carry):
        for u in range(CONV_UNROLL):
            conv_chunk(pl.multiple_of((c * CONV_UNROLL + u) * CONV_ROWS, CONV_ROWS))
        return carry

    lax.fori_loop(0, ls // (CONV_ROWS * CONV_UNROLL), conv, 0)

    ps = ps_ref[...]

    def pool(c, carry):
        r0 = pl.multiple_of(c * ch, ch)
        t = r0 + lax.broadcasted_iota(jnp.int32, (ch, LANES), 0)
        for g, w in enumerate(POOL_WINDOWS):
            lanes = slice(LANES * g, LANES * (g + 1))
            uwin = upad[pl.ds(r0, ch + 2 * PAD_ROWS), lanes]
            s = None
            for o in range(PAD_ROWS - w // 2, PAD_ROWS + w // 2):
                piece = uwin[o:o + ch, :]
                s = piece if s is None else s + piece
            lo = jnp.maximum(t - w // 2, 0)
            hi = jnp.minimum(t + w // 2, ls)
            d = s / (hi - lo).astype(F32) - uwin[PAD_ROWS:PAD_ROWS + ch, :]
            y = jnp.dot(d.astype(BF16), pw_ref[g], preferred_element_type=F32) * ps[:, lanes]
            yc_ref[pl.ds(r0, ch), lanes] = y.astype(BF16)
        return carry

    lax.fori_loop(0, ls // ch, pool, 0)


def _local_mixers(a, p, lw, ls, blk0, nseq):
    seq_in = lambda w: pl.BlockSpec((ls, w), lambda b: (blk0 + b, 0), pipeline_mode=pl.Buffered(1))
    seq_out = pl.BlockSpec((ls, 512), lambda b: (b, 0))
    full = lambda shape: pl.BlockSpec(shape, lambda b: tuple(0 for _ in shape))
    return pl.pallas_call(
        functools.partial(_local_kernel, ls=ls),
        grid=(nseq,),
        in_specs=[seq_in(1024), seq_in(512), full((SUBLANES * 32, CONV_CH)),
                  full((SUBLANES, CONV_ROWS + 2 * PAD_ROWS, CONV_ROWS + 2 * PAD_ROWS)), full((1, CONV_CH)), full((1, CONV_CH)),
                  full((1, CONV_CH)), full((4, POOL_GROUP, POOL_GROUP)), full((1, POOL_CH))],
        out_specs=[seq_out, seq_out],
        out_shape=[jax.ShapeDtypeStruct((nseq * ls, 512), BF16)] * 2,
        scratch_shapes=[pltpu.VMEM((ls + 2 * PAD_ROWS, CONV_CH), F32),
                        pltpu.VMEM((ls + 2 * PAD_ROWS, POOL_CH), F32)],
        compiler_params=_cparams(("parallel",)),
        name="local_mixers",
    )(a, p, lw["conv_w"], lw["conv_shift"], lw["conv_b"], lw["conv_ln_g"], lw["conv_ln_b"], lw["pool_w"],
      lw["pool_scale"])


def _swa_heads(q, kk, vv, bias, sink_ref, o_ref):
    bq = q.shape[0]
    lane = lax.broadcasted_iota(jnp.int32, (bq, LANES), 1)
    low = lane < 64
    zero = jnp.zeros((), BF16)
    one = jnp.ones((), BF16)
    dn = (((1,), (1,)), ((), ()))
    lane_k = lax.broadcasted_iota(jnp.int32, (kk.shape[0], LANES), 1) < 64
    for g in range(2):
        pairs = (q[:, LANES * (2 * g):LANES * (2 * g + 1)], q[:, LANES * (2 * g + 1):LANES * (2 * g + 2)])
        qs = jnp.concatenate([jnp.where(low, pairs[0], zero), jnp.where(low, pairs[1], zero),
                              jnp.where(low, zero, pairs[0]), jnp.where(low, zero, pairs[1])], axis=0)
        s = lax.dot_general(qs, kk[:, LANES * g:LANES * (g + 1)], dn, preferred_element_type=F32)
        if bias is not None:
            s = s + jnp.concatenate([bias] * 4, axis=0)
        heads = (4 * g, 4 * g + 2, 4 * g + 1, 4 * g + 3)
        sk = jnp.concatenate([jnp.broadcast_to(sink_ref[h:h + 1, :], (bq, LANES)) for h in heads], axis=0)
        m = jnp.maximum(jnp.max(s, axis=-1, keepdims=True), sk)
        e_sink = jnp.exp2(sk - m)
        p = jnp.concatenate([jnp.exp2(s[:, LANES * t:LANES * (t + 1)] - m).astype(BF16)
                             for t in range(s.shape[1] // LANES)], axis=1)
        vg = vv[:, LANES * g:LANES * (g + 1)]
        o_even = jnp.dot(p[:2 * bq], jnp.where(lane_k, vg, one), preferred_element_type=F32)
        o_odd = jnp.dot(p[2 * bq:], jnp.where(lane_k, one, vg), preferred_element_type=F32)
        out_even = o_even * pltpu.roll(1.0 / (o_even + e_sink[:2 * bq]), 64, 1)
        out_odd = o_odd * pltpu.roll(1.0 / (o_odd + e_sink[2 * bq:]), 64, 1)
        for jj in range(2):
            rows = slice(jj * bq, (jj + 1) * bq)
            o_ref[:, LANES * (2 * g + jj):LANES * (2 * g + jj + 1)] = jnp.where(
                low, out_even[rows], out_odd[rows]).astype(BF16)


def _swa_lat_kernel(q_ref, kp_ref, kc_ref, kn_ref, vp_ref, vc_ref, vn_ref, kx_ref, vx_ref, sink_ref,
                    o_ref, *, seq_len):
    i = pl.program_id(1)
    kk = jnp.concatenate([kp_ref[...], kc_ref[...], kn_ref[...], kx_ref[...]], axis=0)
    vv = jnp.concatenate([vp_ref[...], vc_ref[...], vn_ref[...], vx_ref[...]], axis=0)
    nk = kk.shape[0]
    r = lax.broadcasted_iota(jnp.int32, (SWA_BLOCK, nk), 0)
    s = lax.broadcasted_iota(jnp.int32, (SWA_BLOCK, nk), 1)
    kpos = (i - 1) * SWA_BLOCK + s
    diff = kpos - (i * SWA_BLOCK + r)
    ok = ((kpos >= 0) & (kpos < seq_len) & (jnp.abs(diff) <= SWA_WINDOW)) | (s >= 3 * SWA_BLOCK)
    _swa_heads(q_ref[...], kk, vv, jnp.where(ok, 0.0, MASK_VALUE), sink_ref, o_ref)


def _swa_ctx_kernel(q_ref, kx_ref, vx_ref, sink_ref, o_ref):
    _swa_heads(q_ref[...], kx_ref[...], vx_ref[...], None, sink_ref, o_ref)


def _swa(q, k2, v2, sink_b, geo, with_ctx):
    nb, ls, lc = geo["B"], geo["L"], geo["C"]
    nq = ls // SWA_BLOCK
    ctx_blk0 = nb * ls // lc
    qspec = pl.BlockSpec((SWA_BLOCK, 512), lambda b, i: (b * nq + i, 0))
    kv = lambda off: pl.BlockSpec(
        (SWA_BLOCK, 256), lambda b, i: (b * nq + jnp.clip(i + off, 0, nq - 1), 0))
    cx = pl.BlockSpec((lc, 256), lambda b, i: (ctx_blk0 + b, 0))
    sk = pl.BlockSpec((SUBLANES, LANES), lambda b, i: (0, 0))
    yb = pl.pallas_call(
        functools.partial(_swa_lat_kernel, seq_len=ls),
        grid=(nb, nq),
        in_specs=[qspec, kv(-1), kv(0), kv(1), kv(-1), kv(0), kv(1), cx, cx, sk],
        out_specs=qspec,
        out_shape=jax.ShapeDtypeStruct((nb * ls, 512), BF16),
        compiler_params=_cparams(("parallel", "parallel")),
        name="swa_latent",
    )(q, k2, k2, k2, v2, v2, v2, k2, v2, sink_b)
    if not with_ctx:
        return yb, None
    cx1 = pl.BlockSpec((lc, 256), lambda b: (ctx_blk0 + b, 0))
    yb_ctx = pl.pallas_call(
        _swa_ctx_kernel,
        grid=(nb,),
        in_specs=[pl.BlockSpec((lc, 512), lambda b: (ctx_blk0 + b, 0)), cx1, cx1,
                  pl.BlockSpec((SUBLANES, LANES), lambda b: (0, 0))],
        out_specs=pl.BlockSpec((lc, 512), lambda b: (b, 0)),
        out_shape=jax.ShapeDtypeStruct((nb * lc, 512), BF16),
        compiler_params=_cparams(("parallel",)),
        name="swa_context",
    )(q, k2, v2, sink_b)
    return yb, yb_ctx


MLA_KC = 512
MLA_ROWS = MLA_HEADS * MLA_BLOCK
MLA_KX = 256
MLA_SOFTMAX_ROWS = 16


def _mla_kernel(cq_ref, *rest, n_lat, n_ctx, kc, kx):
    if n_lat:
        kl_ref, rest = rest[0], rest[1:]
    (kx_ref, qg_ref, wuq_ref, wukp_ref, wuvp_ref, cs_ref, sn_ref, o_ref,
     q_sc, s_a, s_b, p_a, p_b, al_a, al_b, m_sc, acc_sc) = rest
    n = n_lat + n_ctx
    scale = (MLA_NOPE + MLA_ROPE) ** -0.5 * LOG2E
    cq = cq_ref[...].astype(F32)
    cqn = cq * lax.rsqrt(jnp.mean(cq * cq, axis=-1, keepdims=True) + LN_EPS) * qg_ref[...]
    q = jnp.dot(cqn.astype(BF16), wuq_ref[...], preferred_element_type=F32)
    qr = q[:, 512:768]
    lane2 = lax.broadcasted_iota(jnp.int32, (MLA_BLOCK, 2 * LANES), 1)
    rot = jnp.where((lane2 % 32) < 16, pltpu.roll(qr, 2 * LANES - 16, 1), pltpu.roll(qr, 16, 1))
    qr = (qr * cs_ref[...] + rot * sn_ref[...]) * scale
    slot_of_lane = lax.broadcasted_iota(jnp.int32, (MLA_BLOCK, LANES), 1) // 32
    for j in range(MLA_HEADS // 2):
        qa2 = jnp.dot(q[:, LANES * j:LANES * (j + 1)].astype(BF16), wukp_ref[j],
                      preferred_element_type=F32) * scale
        for e in range(2):
            h = 2 * j + e
            rows = slice(h * MLA_BLOCK, (h + 1) * MLA_BLOCK)
            qrh = jnp.where(slot_of_lane == (h % 4), qr[:, LANES * (h // 4):LANES * (h // 4 + 1)], 0.0)
            q_sc[rows, 0:LANES] = qa2[:, LANES * e:LANES * (e + 1)].astype(BF16)
            q_sc[rows, LANES:2 * LANES] = qrh.astype(BF16)

    m_sc[...] = jnp.full(m_sc.shape, MASK_VALUE, F32)
    acc_sc[...] = jnp.zeros(acc_sc.shape, F32)
    dn = (((1,), (1,)), ((), ()))
    s_bufs, p_bufs, al_bufs = (s_a, s_b), (p_a, p_b), (al_a, al_b)

    def width(c):
        return kx if isinstance(c, int) and c >= n_lat else kc

    def keys(c):
        if isinstance(c, int):
            if c >= n_lat:
                return kx_ref[(c - n_lat) * kx:(c - n_lat + 1) * kx, :]
            return kl_ref[c * kc:(c + 1) * kc, :]
        return kl_ref[pl.ds(pl.multiple_of(c * kc, kc), kc), :]

    def scores(c, par):
        s_bufs[par][:, 0:width(c)] = lax.dot_general(q_sc[...], keys(c), dn, preferred_element_type=F32)

    def softmax(c, par):
        s_ref, p_ref, al_ref = s_bufs[par], p_bufs[par], al_bufs[par]
        w = width(c)
        for rb in range(MLA_ROWS // MLA_SOFTMAX_ROWS):
            rows = slice(rb * MLA_SOFTMAX_ROWS, (rb + 1) * MLA_SOFTMAX_ROWS)
            sv = s_ref[rows, 0:w]
            m_old = m_sc[rows, :]
            m_new = jnp.maximum(m_old, jnp.max(sv, axis=-1, keepdims=True))
            al_ref[rows, :] = jnp.exp2(m_old - m_new)
            m_sc[rows, :] = m_new
            for t in range(w // LANES):
                lanes = slice(LANES * t, LANES * (t + 1))
                p_ref[rows, lanes] = jnp.exp2(sv[:, lanes] - m_new).astype(BF16)

    def values(c, par):
        w = width(c)
        v_aug = jnp.concatenate([keys(c)[:, 0:LANES], jnp.ones((w, LANES), BF16)], axis=1)
        pv = jnp.dot(p_bufs[par][:, 0:w], v_aug, preferred_element_type=F32)
        al = al_bufs[par][...]
        acc_sc[...] = jnp.concatenate([al, al], axis=1) * acc_sc[...] + pv

    def step(t, par):
        static = isinstance(t, int)
        if not static or t < n:
            scores(t, par)
        if not static or 1 <= t <= n:
            softmax(t - 1, 1 - par)
        if not static or 2 <= t <= n + 1:
            values(t - 2, par)

    loop_lo = 2
    loop_hi = max(loop_lo, n_lat)
    n_pairs = (loop_hi - loop_lo) // 2
    for t in range(0, loop_lo):
        step(t, t % 2)
    if n_pairs:
        def pair(jp, carry):
            t0 = loop_lo + 2 * jp
            step(t0, 0)
            step(t0 + 1, 1)
            return carry
        lax.fori_loop(0, n_pairs, pair, 0)
    for t in range(loop_lo + 2 * n_pairs, n + 2):
        step(t, t % 2)

    acc = acc_sc[...]
    o = acc[:, 0:LANES] / acc[:, LANES:2 * LANES]
    for j in range(MLA_HEADS // 2):
        o2 = jnp.concatenate([o[(2 * j) * MLA_BLOCK:(2 * j + 1) * MLA_BLOCK, :],
                              o[(2 * j + 1) * MLA_BLOCK:(2 * j + 2) * MLA_BLOCK, :]], axis=1)
        o_ref[:, LANES * j:LANES * (j + 1)] = jnp.dot(
            o2.astype(BF16), wuvp_ref[j], preferred_element_type=F32).astype(BF16)


def _mla(cq, kc_all, lw, tabs, geo, lat):
    nb, ls, lc = geo["B"], geo["L"], geo["C"]
    ctx_blk0 = nb * ls // lc
    if lat:
        nq = ls // MLA_BLOCK
        qmap = lambda b, i: (b * nq + i, 0)
        tmap = lambda b, i: (i, 0)
        kc = min(MLA_KC, ls)
        n_lat = ls // kc
    else:
        nq = lc // MLA_BLOCK
        qmap = lambda b, i: (nb * (ls // MLA_BLOCK) + b * nq + i, 0)
        tmap = lambda b, i: (ls // MLA_BLOCK, 0)
        kc, n_lat = MLA_KX, 0
    kx = min(MLA_KX, lc)
    omap = lambda b, i: (b * nq + i, 0)
    full = lambda shape: pl.BlockSpec(shape, lambda b, i: tuple(0 for _ in shape))
    in_specs = [pl.BlockSpec((MLA_BLOCK, 256), qmap)]
    args = [cq]
    if lat:
        in_specs.append(pl.BlockSpec((ls, 256), lambda b, i: (b, 0)))
        args.append(kc_all)
    in_specs += [pl.BlockSpec((lc, 256), lambda b, i: (ctx_blk0 + b, 0)),
                 full((1, 256)), full((256, 768)), full((4, LANES, 256)), full((4, 256, LANES)),
                 pl.BlockSpec((MLA_BLOCK, 256), tmap), pl.BlockSpec((MLA_BLOCK, 256), tmap)]
    args += [kc_all, lw["mla_q_g"], lw["wuq"], lw["wukp"], lw["wuvp"], tabs["cq"], tabs["sq"]]
    return pl.pallas_call(
        functools.partial(_mla_kernel, n_lat=n_lat, n_ctx=lc // kx, kc=kc, kx=kx),
        grid=(nb, nq),
        in_specs=in_specs,
        out_specs=pl.BlockSpec((MLA_BLOCK, 512), omap),
        out_shape=jax.ShapeDtypeStruct((nb * nq * MLA_BLOCK, 512), BF16),
        scratch_shapes=[pltpu.VMEM((MLA_ROWS, 256), BF16),
                        pltpu.VMEM((MLA_ROWS, kc), F32), pltpu.VMEM((MLA_ROWS, kc), F32),
                        pltpu.VMEM((MLA_ROWS, kc), BF16), pltpu.VMEM((MLA_ROWS, kc), BF16),
                        pltpu.VMEM((MLA_ROWS, LANES), F32), pltpu.VMEM((MLA_ROWS, LANES), F32),
                        pltpu.VMEM((MLA_ROWS, LANES), F32), pltpu.VMEM((MLA_ROWS, 2 * LANES), F32)],
        compiler_params=_cparams(("parallel", "parallel")),
        name="mla_latent" if lat else "mla_context",
    )(*args)


def _merge_kernel(x_ref, *rest, alpha, n_lat_tiles, has_ctx):
    n_y = 2 * N_BRANCH if has_ctx else N_BRANCH
    y_refs, rest = rest[:n_y], rest[n_y:]
    (sh1_ref, sc1_ref, g1_ref, sh2_ref, sc2_ref, wg_ref, bg_ref, wb_ref, wo_ref, l1g_ref, l1b_ref,
     rw_ref, rb_ref, x1_ref, h2_ref, ei_ref, ew_ref) = rest
    tm = x_ref.shape[0]
    is_ctx = pl.program_id(0) >= n_lat_tiles
    x = x_ref[...]
    h = (_ln(x) * (1.0 + sc1_ref[...]) + sh1_ref[...]).astype(BF16)
    merged = None
    for i in range(N_BRANCH):
        yi = y_refs[i][...]
        if has_ctx:
            yi = jnp.where(is_ctx, y_refs[N_BRANCH + i][...], yi)
        gate = _sigmoid(jnp.dot(h, wg_ref[i], preferred_element_type=F32) + bg_ref[i])
        term = gate * jnp.dot(yi, wb_ref[i], preferred_element_type=F32)
        merged = term if merged is None else merged + term
    y = jnp.dot(merged.astype(BF16), wo_ref[...], preferred_element_type=F32)
    x1 = _ln(alpha * x + g1_ref[...] * y) * l1g_ref[...] + l1b_ref[...]
    x1_ref[...] = x1
    h2 = _ln(x1) * (1.0 + sc2_ref[...]) + sh2_ref[...]
    for j in range(ROW_TILES):
        h2_ref[pl.ds(j, tm, stride=ROW_TILES), :] = h2[:, LANES * j:LANES * (j + 1)]
    h2_hi = h2.astype(BF16)
    h2_lo = (h2 - h2_hi.astype(F32)).astype(BF16)
    t = jnp.dot(h2_hi, rw_ref[...], preferred_element_type=F32)
    lg = (t[:, 0:LANES] + t[:, LANES:2 * LANES]
          + jnp.dot(h2_lo, rw_ref[:, 0:LANES], preferred_element_type=F32) + rb_ref[...])
    lane = lax.broadcasted_iota(jnp.int32, (tm, LANES), 1).astype(F32)
    vals, idxs = [], []
    for _ in range(TOP_K):
        mx = jnp.max(lg, axis=-1, keepdims=True)
        ix = jnp.min(jnp.where(lg == mx, lane, float(LANES)), axis=-1, keepdims=True)
        vals.append(mx)
        idxs.append(ix)
        lg = jnp.where(lane == ix, -3.0e38, lg)
    exps = [jnp.exp(v - vals[0]) for v in vals]
    den = exps[0] + exps[1] + exps[2] + exps[3]
    ei = jnp.zeros((tm, LANES), F32)
    ew = jnp.zeros((tm, LANES), F32)
    for k in range(TOP_K):
        ei = jnp.where(lane == float(k), idxs[k], ei)
        ew = jnp.where(lane == float(k), exps[k] / den, ew)
    ei_ref[...] = ei.astype(jnp.int32)
    ew_ref[...] = ew


def _merge(x, ys_lat, ys_ctx, mods, lw, geo, alpha):
    tm, n_lat_tiles, tps, nb = geo["tm"], geo["n_lat_tiles"], geo["tps"], geo["B"]
    d = x.shape[1]
    has_ctx = ys_ctx is not None
    n_tiles = geo["n_tiles"] if has_ctx else n_lat_tiles
    n_ctx_tiles = geo["n_tiles"] - n_lat_tiles
    t_rows = n_tiles * tm

    def cond(i):
        return jnp.where(i < n_lat_tiles, i // tps, nb)

    row = lambda w: pl.BlockSpec((tm, w), lambda i: (i, 0))
    y_lat = pl.BlockSpec((tm, BRANCH_W), lambda i: (jnp.minimum(i, n_lat_tiles - 1), 0))
    y_ctx = pl.BlockSpec((tm, BRANCH_W), lambda i: (jnp.clip(i - n_lat_tiles, 0, n_ctx_tiles - 1), 0))
    y_specs = [y_lat] * N_BRANCH + ([y_ctx] * N_BRANCH if has_ctx else [])
    ys = tuple(ys_lat) + (tuple(ys_ctx) if has_ctx else ())
    mod = pl.BlockSpec((None, 1, d), lambda i: (cond(i), 0, 0))
    full = lambda shape: pl.BlockSpec(shape, lambda i: tuple(0 for _ in shape), pipeline_mode=pl.Buffered(1))
    return pl.pallas_call(
        functools.partial(_merge_kernel, alpha=alpha, n_lat_tiles=n_lat_tiles, has_ctx=has_ctx),
        grid=(n_tiles,),
        name="merge_router",
        in_specs=[row(d)] + y_specs + [mod, mod, mod, mod, mod,
                  full((4, d, d)), full((4, 1, d)), full((4, BRANCH_W, d)), full((d, d)),
                  full((1, d)), full((1, d)), full((d, 2 * LANES)), full((1, LANES))],
        out_specs=[row(d), pl.BlockSpec((tm * ROW_TILES, LANES), lambda i: (i, 0)), row(LANES), row(LANES)],
        out_shape=[jax.ShapeDtypeStruct((t_rows, d), F32),
                   jax.ShapeDtypeStruct((t_rows * ROW_TILES, LANES), F32),
                   jax.ShapeDtypeStruct((t_rows, LANES), jnp.int32),
                   jax.ShapeDtypeStruct((t_rows, LANES), F32)],
        compiler_params=_cparams(("parallel",)),
    )(x, *ys, mods["sh1"], mods["sc1"], mods["g1"], mods["sh2"], mods["sc2"],
      lw["w_gate"], lw["b_gate"], lw["w_branch"], lw["w_out"], lw["ln1_g"], lw["ln1_b"],
      lw["router_w"], lw["router_b"])


GMM_IDX_SLOTS = 8
GMM_X_SLOTS = 3
GMM_CAST_ROWS = 128


def _gmm_kernel(blk_e_ref, nvalid_ref, idx_hbm, h2_hbm, wgu_ref, bgu_ref, wdn_ref, bdn_ref,
                ytk_hbm, idx_sm, xbuf, ybuf, wgu_bf, wdn_bf, sem_i, sem_g, sem_s):
    b = pl.program_id(0)
    n_blocks = pl.num_programs(0)
    nvalid = nvalid_ref[0]
    bm = xbuf.shape[1] // ROW_TILES
    rows = bm * ROW_TILES
    pad0 = ytk_hbm.shape[0] - rows

    def idx_copy(blk):
        slot = blk % GMM_IDX_SLOTS
        return pltpu.make_async_copy(idx_hbm.at[jnp.minimum(blk, n_blocks - 1)], idx_sm.at[slot], sem_i.at[slot])

    def gather_row(blk, r):
        src = pl.multiple_of(idx_sm[blk % GMM_IDX_SLOTS, 0, r] * ROW_TILES, ROW_TILES)
        slot = blk % GMM_X_SLOTS
        pltpu.make_async_copy(h2_hbm.at[pl.ds(src, ROW_TILES), :],
                              xbuf.at[slot, pl.ds(r * ROW_TILES, ROW_TILES), :], sem_g.at[slot]).start()

    def scatter_row(blk, r):
        dst = pl.multiple_of(idx_sm[(blk + GMM_IDX_SLOTS) % GMM_IDX_SLOTS, 1, r] * ROW_TILES, ROW_TILES)
        slot = (blk + 2) % 2
        pltpu.make_async_copy(ybuf.at[slot, pl.ds(r * ROW_TILES, ROW_TILES), :],
                              ytk_hbm.at[pl.ds(dst, ROW_TILES), :], sem_s.at[slot]).start()

    def gather_wait(blk):
        slot = blk % GMM_X_SLOTS
        pltpu.make_async_copy(h2_hbm.at[pl.ds(0, rows), :], xbuf.at[slot], sem_g.at[slot]).wait()

    def scatter_wait(blk):
        slot = (blk + 2) % 2
        pltpu.make_async_copy(ybuf.at[slot], ytk_hbm.at[pl.ds(0, rows), :], sem_s.at[slot]).wait()

    def rolled(fn):
        def body(r, carry):
            fn(r)
            return carry
        lax.fori_loop(0, bm, body, 0)

    @pl.when(b == 0)
    def _():
        ybuf[1] = jnp.zeros(ybuf.shape[1:], F32)

        def fill(r):
            idx_sm[GMM_IDX_SLOTS - 1, 1, r] = pad0 // ROW_TILES + r

        rolled(fill)
        for blk in (0, 1):
            idx_copy(blk).start()
            idx_copy(blk).wait()
        idx_copy(2).start()
        rolled(lambda r: gather_row(0, r))
        rolled(lambda r: gather_row(1, r))

    @pl.when((b < nvalid) & ((b == 0) | (blk_e_ref[b] != blk_e_ref[jnp.maximum(b - 1, 0)])))
    def _():
        def cast(i, carry):
            rws = pl.ds(pl.multiple_of(i * GMM_CAST_ROWS, GMM_CAST_ROWS), GMM_CAST_ROWS)
            wgu_bf[rws, :] = wgu_ref[rws, :].astype(BF16)
            wdn_bf[rws, :] = wdn_ref[rws, :].astype(BF16)
            return carry
        lax.fori_loop(0, wgu_bf.shape[0] // GMM_CAST_ROWS, cast, 0)

    @pl.when(b < nvalid)
    def _():
        par = b % 2
        idx_copy(b + 3).start()
        idx_copy(b + 2).wait()
        gather_wait(b)
        xs = jnp.concatenate(
            [xbuf[b % GMM_X_SLOTS, pl.ds(j, bm, stride=ROW_TILES), :] for j in range(ROW_TILES)],
            axis=1).astype(BF16)
        for r in range(bm):
            gather_row(b + 2, r)
        for r in range(bm):
            scatter_row(b - 1, r)
        z = jnp.dot(xs, wgu_bf[...], preferred_element_type=F32) + bgu_ref[...]
        gate = jnp.minimum(z[:, :D_EXPERT], SWIGLU_LIMIT)
        up = jnp.clip(z[:, D_EXPERT:], -SWIGLU_LIMIT, SWIGLU_LIMIT)
        act = (up + 1.0) * gate * _sigmoid(SWIGLU_ALPHA * gate)
        y = jnp.dot(act.astype(BF16), wdn_bf[...], preferred_element_type=F32) + bdn_ref[...]
        for j in range(ROW_TILES):
            ybuf[par, pl.ds(j, bm, stride=ROW_TILES), :] = y[:, LANES * j:LANES * (j + 1)]
        scatter_wait(b - 1)

    @pl.when(b == nvalid - 1)
    def _():
        rolled(lambda r: scatter_row(b, r))
        scatter_wait(b)
        gather_wait(b + 1)
        gather_wait(b + 2)
        idx_copy(b + 3).wait()


def _gmm(h2g, plan, lw, n_tok):
    bm = EXPERT_BM
    n_blocks = plan["idx"].shape[0]
    d = lw["w_gu"].shape[1]
    n_slots = n_tok * TOP_K + bm
    grid_spec = pltpu.PrefetchScalarGridSpec(
        num_scalar_prefetch=2,
        grid=(n_blocks,),
        in_specs=[
            pl.BlockSpec(memory_space=pl.ANY),
            pl.BlockSpec(memory_space=pl.ANY),
            pl.BlockSpec((None, d, 2 * D_EXPERT), lambda b, be, nv: (be[b], 0, 0)),
            pl.BlockSpec((None, 1, 2 * D_EXPERT), lambda b, be, nv: (be[b], 0, 0)),
            pl.BlockSpec((None, D_EXPERT, d), lambda b, be, nv: (be[b], 0, 0)),
            pl.BlockSpec((None, 1, d), lambda b, be, nv: (be[b], 0, 0)),
        ],
        out_specs=pl.BlockSpec(memory_space=pl.ANY),
        scratch_shapes=[pltpu.SMEM((GMM_IDX_SLOTS, 2, bm), jnp.int32),
                        pltpu.VMEM((GMM_X_SLOTS, bm * ROW_TILES, LANES), F32),
                        pltpu.VMEM((2, bm * ROW_TILES, LANES), F32),
                        pltpu.VMEM((d, 2 * D_EXPERT), BF16),
                        pltpu.VMEM((D_EXPERT, d), BF16),
                        pltpu.SemaphoreType.DMA((GMM_IDX_SLOTS,)),
                        pltpu.SemaphoreType.DMA((GMM_X_SLOTS,)),
                        pltpu.SemaphoreType.DMA((2,))],
    )
    return pl.pallas_call(
        _gmm_kernel,
        grid_spec=grid_spec,
        out_shape=jax.ShapeDtypeStruct((n_slots * ROW_TILES, LANES), F32),
        compiler_params=_cparams(("arbitrary",)),
        name="expert_gmm",
    )(plan["blk_e"], plan["nvalid"], plan["idx"], h2g, lw["w_gu"], lw["b_gu"], lw["w_down"], lw["b_down"])


def _plan(ei, n_tok):
    bm = EXPERT_BM
    n_assign = n_tok * TOP_K
    flat_e = ei.reshape(-1)
    id_bits = max(1, (n_assign - 1).bit_length())
    assert N_EXPERTS << id_bits < 2 ** 31
    packed = jnp.sort((flat_e << id_bits) | jnp.arange(n_assign, dtype=jnp.int32))
    order = packed & ((1 << id_bits) - 1)
    experts = jnp.arange(N_EXPERTS, dtype=jnp.int32)
    counts = jnp.sum((flat_e[:, None] == experts[None, :]).astype(jnp.int32), axis=0)
    padded = (counts + bm - 1) // bm * bm
    pad_end = jnp.cumsum(padded)
    pad_start = pad_end - padded
    grp_start = jnp.cumsum(counts) - counts
    n_blocks = -(-(n_assign + N_EXPERTS * (bm - 1)) // bm)
    blk_first = jnp.arange(n_blocks, dtype=jnp.int32) * bm
    blk_e = jnp.minimum(jnp.sum((pad_end[None, :] <= blk_first[:, None]).astype(jnp.int32), axis=1),
                        N_EXPERTS - 1)
    within = blk_first[:, None] + jnp.arange(bm, dtype=jnp.int32)[None, :] - pad_start[blk_e][:, None]
    valid = (within < counts[blk_e][:, None]) & (blk_first < pad_end[-1])[:, None]
    pos = jnp.clip(grp_start[blk_e][:, None] + within, 0, n_assign - 1)
    row_asg = order[pos]
    row_src = jnp.where(valid, row_asg // TOP_K, 0)
    row_dst = jnp.where(valid, (row_asg % TOP_K) * n_tok + row_asg // TOP_K,
                        n_assign + jnp.arange(bm, dtype=jnp.int32)[None, :])
    idx = jnp.stack([row_src, row_dst], axis=1).astype(jnp.int32)
    nvalid = (pad_end[-1:] // bm).astype(jnp.int32)
    return {"idx": idx, "blk_e": blk_e.astype(jnp.int32), "nvalid": nvalid}


def _combine_kernel(x_ref, y0_ref, y1_ref, y2_ref, y3_ref, ew_ref, g2_ref, lg_ref, lb_ref, o_ref, *, alpha):
    tm = x_ref.shape[0]
    ew = ew_ref[...]
    cols = []
    for j in range(ROW_TILES):
        acc = None
        for k, y_ref in enumerate((y0_ref, y1_ref, y2_ref, y3_ref)):
            piece = y_ref[pl.ds(j, tm, stride=ROW_TILES), :] * ew[:, k:k + 1]
            acc = piece if acc is None else acc + piece
        cols.append(acc)
    m = jnp.concatenate(cols, axis=1)
    o_ref[...] = _ln(alpha * x_ref[...] + g2_ref[...] * m) * lg_ref[...] + lb_ref[...]


def _combine(x1, ytk, ew, g2, lw, geo, n_tiles, alpha):
    tm, n_lat_tiles, tps, nb = geo["tmc"], geo["n_lat_tiles"], geo["tps"], geo["B"]
    ratio = geo["tm"] // tm
    d = x1.shape[1]

    def cond(i):
        return jnp.where(i < n_lat_tiles * ratio, i // (tps * ratio), nb)

    def slot_map(k, i):
        return (k * n_tiles * ratio + i, 0)

    row = lambda w: pl.BlockSpec((tm, w), lambda i: (i, 0))
    full = lambda shape: pl.BlockSpec(shape, lambda i: tuple(0 for _ in shape))
    return pl.pallas_call(
        functools.partial(_combine_kernel, alpha=alpha),
        grid=(n_tiles * ratio,),
        in_specs=[row(d)] + [pl.BlockSpec((tm * ROW_TILES, LANES), functools.partial(slot_map, k))
                             for k in range(TOP_K)] + [row(LANES),
                  pl.BlockSpec((None, 1, d), lambda i: (cond(i), 0, 0)), full((1, d)), full((1, d))],
        out_specs=row(d),
        out_shape=jax.ShapeDtypeStruct((n_tiles * tm * ratio, d), F32),
        compiler_params=_cparams(("parallel",)),
        name="moe_combine",
    )(x1, ytk, ytk, ytk, ytk, ew, g2, lw["ln2_g"], lw["ln2_b"])


def _rope_tables(seq_len, tm):
    rows = seq_len // GRID_W
    row = jnp.repeat(jnp.arange(rows, dtype=F32), GRID_W)
    col = jnp.tile(jnp.arange(GRID_W, dtype=F32), rows)

    def table(rot_dim, reps, pad_rows):
        n_freq = rot_dim // 4
        inv = ROPE_THETA ** (-jnp.arange(n_freq, dtype=F32) / n_freq)
        ang = jnp.concatenate([row[:, None] * inv, col[:, None] * inv], axis=-1)
        cos, sin = jnp.cos(ang), jnp.sin(ang)
        c = jnp.tile(jnp.concatenate([cos, cos], axis=-1), (1, reps))
        s = jnp.tile(jnp.concatenate([-sin, sin], axis=-1), (1, reps))
        c = jnp.concatenate([c, jnp.ones((pad_rows, c.shape[1]), F32)], axis=0)
        s = jnp.concatenate([s, jnp.zeros((pad_rows, s.shape[1]), F32)], axis=0)
        return c, s

    cs, sn = table(HEAD_DIM, 2, tm)
    ck, sk = table(MLA_ROPE, 4, tm)
    cq, sq = table(MLA_ROPE, 8, MLA_BLOCK)
    return {"cs": cs, "sn": sn, "ck": ck, "sk": sk, "cq": cq, "sq": sq}


def _layer_weights(l, w_in, conv_w, conv_b, conv_ln_g, conv_ln_b, swa_sink, pool_w, pool_scale, mla_q_g,
                   mla_w_uq, mla_kv_g, mla_w_uk, mla_w_uv, w_branch, w_gate, b_gate, w_out, ln1_g, ln1_b,
                   router_w, router_b, w_gu, b_gu, w_down, b_down, ln2_g, ln2_b):
    d = w_in.shape[1]
    wi = w_in[l]
    kr = wi[:, 2688:2720]
    w_all = jnp.concatenate([wi[:, :2688], kr, kr, kr, kr], axis=1).astype(BF16)
    wuq = mla_w_uq[l].reshape(MLA_Q_RANK, MLA_HEADS, MLA_NOPE + MLA_ROPE)
    wuq = jnp.concatenate([wuq[:, :, :MLA_NOPE].reshape(MLA_Q_RANK, -1),
                           wuq[:, :, MLA_NOPE:].reshape(MLA_Q_RANK, -1)], axis=1).astype(BF16)
    wuk = jnp.transpose(mla_w_uk[l], (1, 2, 0))
    wuv = jnp.transpose(mla_w_uv[l], (1, 0, 2))
    zk = jnp.zeros((MLA_NOPE, MLA_KV_RANK), F32)
    zv = jnp.zeros((MLA_KV_RANK, MLA_V), F32)
    wukp = jnp.stack([jnp.block([[wuk[2 * j], zk], [zk, wuk[2 * j + 1]]]) for j in range(MLA_HEADS // 2)])
    wuvp = jnp.stack([jnp.block([[wuv[2 * j], zv], [zv, wuv[2 * j + 1]]]) for j in range(MLA_HEADS // 2)])
    win_i = jnp.arange(CONV_ROWS + 2 * PAD_ROWS)
    rw = jnp.zeros((d, LANES), F32).at[:, :N_EXPERTS].set(router_w[l])
    rw_hi = rw.astype(BF16)
    rb = jnp.full((1, LANES), MASK_VALUE, F32).at[0, :N_EXPERTS].set(router_b[l])
    return {
        "w_all": w_all,
        "conv_w": jnp.repeat(jnp.concatenate([conv_w[l], jnp.zeros((1, CONV_CH), F32)], axis=0), SUBLANES, axis=0),
        "conv_shift": (win_i[None, None, :] == win_i[None, :, None] + jnp.arange(SUBLANES)[:, None, None]).astype(BF16),
        "conv_b": conv_b[l][None], "conv_ln_g": conv_ln_g[l][None], "conv_ln_b": conv_ln_b[l][None],
        "sink": jnp.broadcast_to(swa_sink[l][:, None] * LOG2E, (SWA_HEADS, LANES)),
        "pool_w": pool_w[l].astype(BF16), "pool_scale": pool_scale[l][None],
        "mla_q_g": mla_q_g[l][None], "mla_kv_g": mla_kv_g[l][None],
        "wuq": wuq, "wukp": wukp.astype(BF16), "wuvp": wuvp.astype(BF16),
        "w_branch": w_branch[l].astype(BF16), "w_gate": w_gate[l].astype(BF16),
        "b_gate": b_gate[l][:, None, :], "w_out": w_out[l].astype(BF16),
        "ln1_g": ln1_g[l][None], "ln1_b": ln1_b[l][None],
        "router_w": jnp.concatenate([rw_hi, (rw - rw_hi.astype(F32)).astype(BF16)], axis=1), "router_b": rb,
        "w_gu": w_gu[l], "b_gu": b_gu[l][:, None, :],
        "w_down": w_down[l], "b_down": b_down[l][:, None, :],
        "ln2_g": ln2_g[l][None], "ln2_b": ln2_b[l][None],
    }


def _token_tile(seq_len, ctx_rows):
    for tm in (512, 256, 128):
        if seq_len % tm == 0 and ctx_rows % tm == 0:
            return tm
    raise ValueError("unsupported sequence / context lengths")


def kernel(x, c, ctx, c_ctx, w_ada, b_ada, w_in, conv_w, conv_b, conv_ln_g, conv_ln_b, swa_sink, pool_w,
           pool_scale, mla_q_g, mla_w_uq, mla_kv_g, mla_w_uk, mla_w_uv, w_branch, w_gate, b_gate, w_out,
           ln1_g, ln1_b, router_w, router_b, w_gu, b_gu, w_down, b_down, ln2_g, ln2_b):
    nb, ls, d = x.shape
    lc = ctx.shape[1]
    depth = w_ada.shape[0]
    assert ls % GRID_W == 0 and ls % SWA_BLOCK == 0 and (nb * ls) % lc == 0
    assert ls % min(MLA_KC, ls) == 0 and lc % min(MLA_KX, lc) == 0 and lc % MLA_BLOCK == 0
    tm = _token_tile(ls, nb * lc)
    t_lat, t_ctx = nb * ls, nb * lc
    geo = {"B": nb, "L": ls, "C": lc, "tm": tm, "tmc": min(tm, 256), "tps": ls // tm,
           "n_lat_tiles": t_lat // tm, "n_tiles": (t_lat + t_ctx) // tm}
    alpha = (2 * depth) ** 0.25

    ra = -(-(nb + 1) // SUBLANES) * SUBLANES
    cvec = jnp.zeros((ra, d), F32).at[:nb].set(c).at[nb].set(c_ctx)
    ada = _ada_terms(cvec, w_ada, b_ada).reshape(depth, ra, 6, 1, d)
    tabs = _rope_tables(ls, tm)

    xa = jnp.concatenate([x.reshape(t_lat, d), ctx.reshape(t_ctx, d)], axis=0)
    for l in range(depth):
        ctx_out = l < depth - 1
        n_tiles = geo["n_tiles"] if ctx_out else geo["n_lat_tiles"]
        names = ("sh1", "sc1", "g1", "sh2", "sc2", "g2")
        mods = {n: ada[l, :, i] for i, n in enumerate(names)}
        lw = _layer_weights(l, w_in, conv_w, conv_b, conv_ln_g, conv_ln_b, swa_sink, pool_w, pool_scale,
                            mla_q_g, mla_w_uq, mla_kv_g, mla_w_uk, mla_w_uv, w_branch, w_gate, b_gate,
                            w_out, ln1_g, ln1_b, router_w, router_b, w_gu, b_gu, w_down, b_down,
                            ln2_g, ln2_b)
        a, q, k2, v2, p, cq, kc = _in_proj(xa, mods["sh1"], mods["sc1"], lw["w_all"], tabs,
                                           lw["mla_kv_g"], geo)
        ya, yc = _local_mixers(a, p, lw, ls, 0, nb)
        yb, yb_ctx = _swa(q, k2, v2, lw["sink"], geo, ctx_out)
        yd = _mla(cq, kc, lw, tabs, geo, True)
        ys_ctx = None
        if ctx_out:
            ya_ctx, yc_ctx = _local_mixers(a, p, lw, lc, t_lat // lc, nb)
            yd_ctx = _mla(cq, kc, lw, tabs, geo, False)
            ys_ctx = (ya_ctx, yb_ctx, yc_ctx, yd_ctx)
        x1, h2g, ei, ew = _merge(xa, (ya, yb, yc, yd), ys_ctx, mods, lw, geo, alpha)
        n_tok = n_tiles * tm
        plan = _plan(ei[:, :TOP_K], n_tok)
        ytk = _gmm(h2g, plan, lw, n_tok)
        xa = _combine(x1, ytk, ew, mods["g2"], lw, geo, n_tiles, alpha)
    return xa[:t_lat].reshape(nb, ls, d)
```

```python
import functools

import jax
import jax.numpy as jnp
import numpy as np
from jax import lax
from jax.experimental import pallas as pl
from jax.experimental.pallas import tpu as pltpu

F32 = jnp.float32
BF16 = jnp.bfloat16
HIGHEST = lax.Precision.HIGHEST

GRID_W = 64
HEAD_DIM = 64
ROPE_THETA = 10000.0
LN_EPS = 1e-6
MASK_VALUE = -1e30
CONV_CH = 512
CONV_K = 31
SWA_HEADS = 8
SWA_WINDOW = 128
SWA_BLOCK = 128
POOL_CH = 512
POOL_WINDOWS = (2, 4, 8, 16)
POOL_GROUP = 128
MLA_HEADS = 8
MLA_Q_RANK = 256
MLA_KV_RANK = 128
MLA_NOPE = 64
MLA_ROPE = 32
MLA_V = 64
MLA_BLOCK = 128
N_BRANCH = 4
BRANCH_W = 512
N_EXPERTS = 32
TOP_K = 4
D_EXPERT = 1024
SWIGLU_LIMIT = 7.0
SWIGLU_ALPHA = 1.702

LOG2E = 1.4426950408889634

LANES = 128
SUBLANES = 8
ROW_TILES = 8
EXPERT_BM = 512
VMEM_LIMIT = 56 * 1024 * 1024


def _cparams(sem):
    return pltpu.CompilerParams(dimension_semantics=sem, vmem_limit_bytes=VMEM_LIMIT)


def _ln(x):
    mu = jnp.mean(x, axis=-1, keepdims=True)
    xc = x - mu
    var = jnp.mean(xc * xc, axis=-1, keepdims=True)
    return xc * lax.rsqrt(var + LN_EPS)


def _sigmoid(x):
    return 0.5 * jnp.tanh(0.5 * x) + 0.5


def _ada_kernel(c_ref, w_ref, b_ref, o_ref):
    c = c_ref[...]
    s = c * _sigmoid(c)
    o_ref[...] = jnp.dot(s, w_ref[...], precision=HIGHEST, preferred_element_type=F32) + b_ref[...]


def _ada_terms(cvec, w_ada, b_ada):
    depth, d, n = w_ada.shape
    ra = cvec.shape[0]
    tn = 1536
    return pl.pallas_call(
        _ada_kernel,
        grid=(depth, n // tn),
        in_specs=[
            pl.BlockSpec((ra, d), lambda l, j: (0, 0)),
            pl.BlockSpec((None, d, tn), lambda l, j: (l, 0, j)),
            pl.BlockSpec((None, 1, tn), lambda l, j: (l, 0, j)),
        ],
        out_specs=pl.BlockSpec((None, ra, tn), lambda l, j: (l, 0, j)),
        out_shape=jax.ShapeDtypeStruct((depth, ra, n), F32),
        compiler_params=_cparams(("parallel", "parallel")),
        name="ada_terms",
    )(cvec, w_ada, b_ada.reshape(depth, 1, n))


IN_COLS = 1024 + 512 + 128 + 128 + 512 + 256 + 128 + 128


def _in_kernel(x_ref, sh_ref, sc_ref, w_ref, cs_ref, sn_ref, ck_ref, sk_ref, kvg_ref,
               a_ref, q_ref, k2_ref, v2_ref, p_ref, cq_ref, kc_ref):
    tm = x_ref.shape[0]
    h = _ln(x_ref[...]) * (1.0 + sc_ref[...]) + sh_ref[...]
    z = jnp.dot(h.astype(BF16), w_ref[...], preferred_element_type=F32)
    a_ref[...] = z[:, 0:1024].astype(BF16)
    cs, sn = cs_ref[...], sn_ref[...]
    lane = lax.broadcasted_iota(jnp.int32, (tm, LANES), 1)
    first32 = (lane % 64) < 32

    def rope64(t):
        rot = jnp.where(first32, pltpu.roll(t, LANES - 32, 1), pltpu.roll(t, 32, 1))
        return t * cs + rot * sn

    for j in range(4):
        qj = rope64(z[:, 1024 + LANES * j:1024 + LANES * (j + 1)]) * (HEAD_DIM ** -0.5 * LOG2E)
        q_ref[:, LANES * j:LANES * (j + 1)] = qj.astype(BF16)
    low64 = lane < 64
    k = rope64(z[:, 1536:1664])
    ks = pltpu.roll(k, 64, 1)
    k2_ref[:, 0:LANES] = jnp.where(low64, k, ks).astype(BF16)
    k2_ref[:, LANES:2 * LANES] = jnp.where(low64, ks, k).astype(BF16)
    v = z[:, 1664:1792]
    vs = pltpu.roll(v, 64, 1)
    v2_ref[:, 0:LANES] = jnp.where(low64, v, vs).astype(BF16)
    v2_ref[:, LANES:2 * LANES] = jnp.where(low64, vs, v).astype(BF16)
    p_ref[...] = z[:, 1792:2304].astype(BF16)
    cq_ref[...] = z[:, 2304:2560].astype(BF16)
    ckv = z[:, 2560:2688]
    ckvn = ckv * lax.rsqrt(jnp.mean(ckv * ckv, axis=-1, keepdims=True) + LN_EPS) * kvg_ref[...]
    kr4 = z[:, 2688:2816]
    first16 = (lane % 32) < 16
    rot = jnp.where(first16, pltpu.roll(kr4, LANES - 16, 1), pltpu.roll(kr4, 16, 1))
    kr4 = kr4 * ck_ref[...] + rot * sk_ref[...]
    kc_ref[:, 0:LANES] = ckvn.astype(BF16)
    kc_ref[:, LANES:2 * LANES] = kr4.astype(BF16)


def _in_proj(x, sh, sc, w_all, tabs, kvg, geo):
    tm, n_tiles, n_lat_tiles, tps, nb = geo["tm"], geo["n_tiles"], geo["n_lat_tiles"], geo["tps"], geo["B"]
    t_rows, d = n_tiles * tm, x.shape[1]

    def cond(i):
        return jnp.where(i < n_lat_tiles, i // tps, nb)

    def pos(i):
        return jnp.where(i < n_lat_tiles, i % tps, tps)

    row = lambda w: pl.BlockSpec((tm, w), lambda i: (i, 0))
    mod = pl.BlockSpec((None, 1, d), lambda i: (cond(i), 0, 0))
    tab = pl.BlockSpec((tm, LANES), lambda i: (pos(i), 0))
    widths = (1024, 512, 256, 256, 512, 256, 256)
    return pl.pallas_call(
        _in_kernel,
        grid=(n_tiles,),
        in_specs=[row(d), mod, mod,
                  pl.BlockSpec((d, IN_COLS), lambda i: (0, 0)),
                  tab, tab, tab, tab,
                  pl.BlockSpec((1, LANES), lambda i: (0, 0))],
        out_specs=[row(w) for w in widths],
        out_shape=[jax.ShapeDtypeStruct((t_rows, w), BF16) for w in widths],
        compiler_params=_cparams(("parallel",)),
        name="in_proj",
    )(x, sh, sc, w_all, tabs["cs"], tabs["sn"], tabs["ck"], tabs["sk"], kvg)


CONV_ROWS = 32
CONV_UNROLL = 4
PAD_ROWS = 16


def _local_kernel(a_ref, p_ref, cw_ref, sh_ref, cb_ref, lg_ref, lb_ref, pw_ref, ps_ref, ya_ref, yc_ref, vpad, upad,
                  *, ls):
    ch = min(256, ls)
    zeros = jnp.zeros((PAD_ROWS, CONV_CH), F32)
    vpad[0:PAD_ROWS, :] = zeros
    vpad[ls + PAD_ROWS:ls + 2 * PAD_ROWS, :] = zeros
    upad[0:PAD_ROWS, :] = zeros
    upad[ls + PAD_ROWS:ls + 2 * PAD_ROWS, :] = zeros

    def fill(c, carry):
        r0 = pl.multiple_of(c * ch, ch)
        a = a_ref[pl.ds(r0, ch), :].astype(F32)
        vpad[pl.ds(r0 + PAD_ROWS, ch), :] = a[:, :CONV_CH] * _sigmoid(a[:, CONV_CH:])
        upad[pl.ds(r0 + PAD_ROWS, ch), :] = p_ref[pl.ds(r0, ch), :].astype(F32)
        return carry

    lax.fori_loop(0, ls // ch, fill, 0)

    cb, lg, lb = cb_ref[...], lg_ref[...], lb_ref[...]
    groups = CONV_ROWS // SUBLANES

    def conv_chunk(r0):
        win = vpad[pl.ds(r0, CONV_ROWS + 2 * PAD_ROWS), :].astype(BF16)
        accs = [jnp.zeros((SUBLANES, CONV_CH), F32) for _ in range(groups)]
        for r in range(SUBLANES):
            wr = win.astype(F32) if r == 0 else jnp.dot(sh_ref[r], win, preferred_element_type=F32)
            for a in range(4):
                k = SUBLANES * a + r - 1
                if 0 <= k < CONV_K:
                    w8 = cw_ref[SUBLANES * k:SUBLANES * (k + 1), :]
                    for g in range(groups):
                        lo = SUBLANES * (a + g)
                        accs[g] = accs[g] + wr[lo:lo + SUBLANES, :] * w8
        y = _ln(jnp.concatenate(accs, axis=0) + cb) * lg + lb
        ya_ref[pl.ds(r0, CONV_ROWS), :] = (y * _sigmoid(y)).astype(BF16)

    def conv(c, carry):
        for u in range(CONV_UNROLL):
            conv_chunk(pl.multiple_of((c * CONV_UNROLL + u) * CONV_ROWS, CONV_ROWS))
        return carry

    lax.fori_loop(0, ls // (CONV_ROWS * CONV_UNROLL), conv, 0)

    ps = ps_ref[...]

    def pool(c, carry):
        r0 = pl.multiple_of(c * ch, ch)
        t = r0 + lax.broadcasted_iota(jnp.int32, (ch, LANES), 0)
        for g, w in enumerate(POOL_WINDOWS):
            lanes = slice(LANES * g, LANES * (g + 1))
            uwin = upad[pl.ds(r0, ch + 2 * PAD_ROWS), lanes]
            s = None
            for o in range(PAD_ROWS - w // 2, PAD_ROWS + w // 2):
                piece = uwin[o:o + ch, :]
                s = piece if s is None else s + piece
            lo = jnp.maximum(t - w // 2, 0)
            hi = jnp.minimum(t + w // 2, ls)
            d = s / (hi - lo).astype(F32) - uwin[PAD_ROWS:PAD_ROWS + ch, :]
            y = jnp.dot(d.astype(BF16), pw_ref[g], preferred_element_type=F32) * ps[:, lanes]
            yc_ref[pl.ds(r0, ch), lanes] = y.astype(BF16)
        return carry

    lax.fori_loop(0, ls // ch, pool, 0)


def _local_mixers(a, p, lw, ls, blk0, nseq):
    seq_in = lambda w: pl.BlockSpec((ls, w), lambda b: (blk0 + b, 0), pipeline_mode=pl.Buffered(1))
    seq_out = pl.BlockSpec((ls, 512), lambda b: (b, 0))
    full = lambda shape: pl.BlockSpec(shape, lambda b: tuple(0 for _ in shape))
    return pl.pallas_call(
        functools.partial(_local_kernel, ls=ls),
        grid=(nseq,),
        in_specs=[seq_in(1024), seq_in(512), full((SUBLANES * 32, CONV_CH)),
                  full((SUBLANES, CONV_ROWS + 2 * PAD_ROWS, CONV_ROWS + 2 * PAD_ROWS)), full((1, CONV_CH)), full((1, CONV_CH)),
                  full((1, CONV_CH)), full((4, POOL_GROUP, POOL_GROUP)), full((1, POOL_CH))],
        out_specs=[seq_out, seq_out],
        out_shape=[jax.ShapeDtypeStruct((nseq * ls, 512), BF16)] * 2,
        scratch_shapes=[pltpu.VMEM((ls + 2 * PAD_ROWS, CONV_CH), F32),
                        pltpu.VMEM((ls + 2 * PAD_ROWS, POOL_CH), F32)],
        compiler_params=_cparams(("parallel",)),
        name="local_mixers",
    )(a, p, lw["conv_w"], lw["conv_shift"], lw["conv_b"], lw["conv_ln_g"], lw["conv_ln_b"], lw["pool_w"],
      lw["pool_scale"])


def _swa_heads(q, kk, vv, bias, sink_ref, o_ref):
    bq = q.shape[0]
    lane = lax.broadcasted_iota(jnp.int32, (bq, LANES), 1)
    low = lane < 64
    zero = jnp.zeros((), BF16)
    one = jnp.ones((), BF16)
    dn = (((1,), (1,)), ((), ()))
    lane_k = lax.broadcasted_iota(jnp.int32, (kk.shape[0], LANES), 1) < 64
    for g in range(2):
        pairs = (q[:, LANES * (2 * g):LANES * (2 * g + 1)], q[:, LANES * (2 * g + 1):LANES * (2 * g + 2)])
        qs = jnp.concatenate([jnp.where(low, pairs[0], zero), jnp.where(low, pairs[1], zero),
                              jnp.where(low, zero, pairs[0]), jnp.where(low, zero, pairs[1])], axis=0)
        s = lax.dot_general(qs, kk[:, LANES * g:LANES * (g + 1)], dn, preferred_element_type=F32)
        if bias is not None:
            s = s + jnp.concatenate([bias] * 4, axis=0)
        heads = (4 * g, 4 * g + 2, 4 * g + 1, 4 * g + 3)
        sk = jnp.concatenate([jnp.broadcast_to(sink_ref[h:h + 1, :], (bq, LANES)) for h in heads], axis=0)
        m = jnp.maximum(jnp.max(s, axis=-1, keepdims=True), sk)
        e_sink = jnp.exp2(sk - m)
        p = jnp.concatenate([jnp.exp2(s[:, LANES * t:LANES * (t + 1)] - m).astype(BF16)
                             for t in range(s.shape[1] // LANES)], axis=1)
        vg = vv[:, LANES * g:LANES * (g + 1)]
        o_even = jnp.dot(p[:2 * bq], jnp.where(lane_k, vg, one), preferred_element_type=F32)
        o_odd = jnp.dot(p[2 * bq:], jnp.where(lane_k, one, vg), preferred_element_type=F32)
        out_even = o_even * pltpu.roll(1.0 / (o_even + e_sink[:2 * bq]), 64, 1)
        out_odd = o_odd * pltpu.roll(1.0 / (o_odd + e_sink[2 * bq:]), 64, 1)
        for jj in range(2):
            rows = slice(jj * bq, (jj + 1) * bq)
            o_ref[:, LANES * (2 * g + jj):LANES * (2 * g + jj + 1)] = jnp.where(
                low, out_even[rows], out_odd[rows]).astype(BF16)


def _swa_lat_kernel(q_ref, kp_ref, kc_ref, kn_ref, vp_ref, vc_ref, vn_ref, kx_ref, vx_ref, sink_ref,
                    o_ref, *, seq_len):
    i = pl.program_id(1)
    kk = jnp.concatenate([kp_ref[...], kc_ref[...], kn_ref[...], kx_ref[...]], axis=0)
    vv = jnp.concatenate([vp_ref[...], vc_ref[...], vn_ref[...], vx_ref[...]], axis=0)
    nk = kk.shape[0]
    r = lax.broadcasted_iota(jnp.int32, (SWA_BLOCK, nk), 0)
    s = lax.broadcasted_iota(jnp.int32, (SWA_BLOCK, nk), 1)
    kpos = (i - 1) * SWA_BLOCK + s
    diff = kpos - (i * SWA_BLOCK + r)
    ok = ((kpos >= 0) & (kpos < seq_len) & (jnp.abs(diff) <= SWA_WINDOW)) | (s >= 3 * SWA_BLOCK)
    _swa_heads(q_ref[...], kk, vv, jnp.where(ok, 0.0, MASK_VALUE), sink_ref, o_ref)


def _swa_ctx_kernel(q_ref, kx_ref, vx_ref, sink_ref, o_ref):
    _swa_heads(q_ref[...], kx_ref[...], vx_ref[...], None, sink_ref, o_ref)


def _swa(q, k2, v2, sink_b, geo, with_ctx):
    nb, ls, lc = geo["B"], geo["L"], geo["C"]
    nq = ls // SWA_BLOCK
    ctx_blk0 = nb * ls // lc
    qspec = pl.BlockSpec((SWA_BLOCK, 512), lambda b, i: (b * nq + i, 0))
    kv = lambda off: pl.BlockSpec(
        (SWA_BLOCK, 256), lambda b, i: (b * nq + jnp.clip(i + off, 0, nq - 1), 0))
    cx = pl.BlockSpec((lc, 256), lambda b, i: (ctx_blk0 + b, 0))
    sk = pl.BlockSpec((SUBLANES, LANES), lambda b, i: (0, 0))
    yb = pl.pallas_call(
        functools.partial(_swa_lat_kernel, seq_len=ls),
        grid=(nb, nq),
        in_specs=[qspec, kv(-1), kv(0), kv(1), kv(-1), kv(0), kv(1), cx, cx, sk],
        out_specs=qspec,
        out_shape=jax.ShapeDtypeStruct((nb * ls, 512), BF16),
        compiler_params=_cparams(("parallel", "parallel")),
        name="swa_latent",
    )(q, k2, k2, k2, v2, v2, v2, k2, v2, sink_b)
    if not with_ctx:
        return yb, None
    cx1 = pl.BlockSpec((lc, 256), lambda b: (ctx_blk0 + b, 0))
    yb_ctx = pl.pallas_call(
        _swa_ctx_kernel,
        grid=(nb,),
        in_specs=[pl.BlockSpec((lc, 512), lambda b: (ctx_blk0 + b, 0)), cx1, cx1,
                  pl.BlockSpec((SUBLANES, LANES), lambda b: (0, 0))],
        out_specs=pl.BlockSpec((lc, 512), lambda b: (b, 0)),
        out_shape=jax.ShapeDtypeStruct((nb * lc, 512), BF16),
        compiler_params=_cparams(("parallel",)),
        name="swa_context",
    )(q, k2, v2, sink_b)
    return yb, yb_ctx


MLA_KC = 512
MLA_ROWS = MLA_HEADS * MLA_BLOCK
MLA_KX = 256
MLA_SOFTMAX_ROWS = 16


def _mla_kernel(cq_ref, *rest, n_lat, n_ctx, kc, kx):
    if n_lat:
        kl_ref, rest = rest[0], rest[1:]
    (kx_ref, qg_ref, wuq_ref, wukp_ref, wuvp_ref, cs_ref, sn_ref, o_ref,
     q_sc, s_a, s_b, p_a, p_b, al_a, al_b, m_sc, acc_sc) = rest
    n = n_lat + n_ctx
    scale = (MLA_NOPE + MLA_ROPE) ** -0.5 * LOG2E
    cq = cq_ref[...].astype(F32)
    cqn = cq * lax.rsqrt(jnp.mean(cq * cq, axis=-1, keepdims=True) + LN_EPS) * qg_ref[...]
    q = jnp.dot(cqn.astype(BF16), wuq_ref[...], preferred_element_type=F32)
    qr = q[:, 512:768]
    lane2 = lax.broadcasted_iota(jnp.int32, (MLA_BLOCK, 2 * LANES), 1)
    rot = jnp.where((lane2 % 32) < 16, pltpu.roll(qr, 2 * LANES - 16, 1), pltpu.roll(qr, 16, 1))
    qr = (qr * cs_ref[...] + rot * sn_ref[...]) * scale
    slot_of_lane = lax.broadcasted_iota(jnp.int32, (MLA_BLOCK, LANES), 1) // 32
    for j in range(MLA_HEADS // 2):
        qa2 = jnp.dot(q[:, LANES * j:LANES * (j + 1)].astype(BF16), wukp_ref[j],
                      preferred_element_type=F32) * scale
        for e in range(2):
            h = 2 * j + e
            rows = slice(h * MLA_BLOCK, (h + 1) * MLA_BLOCK)
            qrh = jnp.where(slot_of_lane == (h % 4), qr[:, LANES * (h // 4):LANES * (h // 4 + 1)], 0.0)
            q_sc[rows, 0:LANES] = qa2[:, LANES * e:LANES * (e + 1)].astype(BF16)
            q_sc[rows, LANES:2 * LANES] = qrh.astype(BF16)

    m_sc[...] = jnp.full(m_sc.shape, MASK_VALUE, F32)
    acc_sc[...] = jnp.zeros(acc_sc.shape, F32)
    dn = (((1,), (1,)), ((), ()))
    s_bufs, p_bufs, al_bufs = (s_a, s_b), (p_a, p_b), (al_a, al_b)

    def width(c):
        return kx if isinstance(c, int) and c >= n_lat else kc

    def keys(c):
        if isinstance(c, int):
            if c >= n_lat:
                return kx_ref[(c - n_lat) * kx:(c - n_lat + 1) * kx, :]
            return kl_ref[c * kc:(c + 1) * kc, :]
        return kl_ref[pl.ds(pl.multiple_of(c * kc, kc), kc), :]

    def scores(c, par):
        s_bufs[par][:, 0:width(c)] = lax.dot_general(q_sc[...], keys(c), dn, preferred_element_type=F32)

    def softmax(c, par):
        s_ref, p_ref, al_ref = s_bufs[par], p_bufs[par], al_bufs[par]
        w = width(c)
        for rb in range(MLA_ROWS // MLA_SOFTMAX_ROWS):
            rows = slice(rb * MLA_SOFTMAX_ROWS, (rb + 1) * MLA_SOFTMAX_ROWS)
            sv = s_ref[rows, 0:w]
            m_old = m_sc[rows, :]
            m_new = jnp.maximum(m_old, jnp.max(sv, axis=-1, keepdims=True))
            al_ref[rows, :] = jnp.exp2(m_old - m_new)
            m_sc[rows, :] = m_new
            for t in range(w // LANES):
                lanes = slice(LANES * t, LANES * (t + 1))
                p_ref[rows, lanes] = jnp.exp2(sv[:, lanes] - m_new).astype(BF16)

    def values(c, par):
        w = width(c)
        v_aug = jnp.concatenate([keys(c)[:, 0:LANES], jnp.ones((w, LANES), BF16)], axis=1)
        pv = jnp.dot(p_bufs[par][:, 0:w], v_aug, preferred_element_type=F32)
        al = al_bufs[par][...]
        acc_sc[...] = jnp.concatenate([al, al], axis=1) * acc_sc[...] + pv

    def step(t, par):
        static = isinstance(t, int)
        if not static or t < n:
            scores(t, par)
        if not static or 1 <= t <= n:
            softmax(t - 1, 1 - par)
        if not static or 2 <= t <= n + 1:
            values(t - 2, par)

    loop_lo = 2
    loop_hi = max(loop_lo, n_lat)
    n_pairs = (loop_hi - loop_lo) // 2
    for t in range(0, loop_lo):
        step(t, t % 2)
    if n_pairs:
        def pair(jp, carry):
            t0 = loop_lo + 2 * jp
            step(t0, 0)
            step(t0 + 1, 1)
            return carry
        lax.fori_loop(0, n_pairs, pair, 0)
    for t in range(loop_lo + 2 * n_pairs, n + 2):
        step(t, t % 2)

    acc = acc_sc[...]
    o = acc[:, 0:LANES] / acc[:, LANES:2 * LANES]
    for j in range(MLA_HEADS // 2):
        o2 = jnp.concatenate([o[(2 * j) * MLA_BLOCK:(2 * j + 1) * MLA_BLOCK, :],
                              o[(2 * j + 1) * MLA_BLOCK:(2 * j + 2) * MLA_BLOCK, :]], axis=1)
        o_ref[:, LANES * j:LANES * (j + 1)] = jnp.dot(
            o2.astype(BF16), wuvp_ref[j], preferred_element_type=F32).astype(BF16)


def _mla(cq, kc_all, lw, tabs, geo, lat):
    nb, ls, lc = geo["B"], geo["L"], geo["C"]
    ctx_blk0 = nb * ls // lc
    if lat:
        nq = ls // MLA_BLOCK
        qmap = lambda b, i: (b * nq + i, 0)
        tmap = lambda b, i: (i, 0)
        kc = min(MLA_KC, ls)
        n_lat = ls // kc
    else:
        nq = lc // MLA_BLOCK
        qmap = lambda b, i: (nb * (ls // MLA_BLOCK) + b * nq + i, 0)
        tmap = lambda b, i: (ls // MLA_BLOCK, 0)
        kc, n_lat = MLA_KX, 0
    kx = min(MLA_KX, lc)
    omap = lambda b, i: (b * nq + i, 0)
    full = lambda shape: pl.BlockSpec(shape, lambda b, i: tuple(0 for _ in shape))
    in_specs = [pl.BlockSpec((MLA_BLOCK, 256), qmap)]
    args = [cq]
    if lat:
        in_specs.append(pl.BlockSpec((ls, 256), lambda b, i: (b, 0)))
        args.append(kc_all)
    in_specs += [pl.BlockSpec((lc, 256), lambda b, i: (ctx_blk0 + b, 0)),
                 full((1, 256)), full((256, 768)), full((4, LANES, 256)), full((4, 256, LANES)),
                 pl.BlockSpec((MLA_BLOCK, 256), tmap), pl.BlockSpec((MLA_BLOCK, 256), tmap)]
    args += [kc_all, lw["mla_q_g"], lw["wuq"], lw["wukp"], lw["wuvp"], tabs["cq"], tabs["sq"]]
    return pl.pallas_call(
        functools.partial(_mla_kernel, n_lat=n_lat, n_ctx=lc // kx, kc=kc, kx=kx),
        grid=(nb, nq),
        in_specs=in_specs,
        out_specs=pl.BlockSpec((MLA_BLOCK, 512), omap),
        out_shape=jax.ShapeDtypeStruct((nb * nq * MLA_BLOCK, 512), BF16),
        scratch_shapes=[pltpu.VMEM((MLA_ROWS, 256), BF16),
                        pltpu.VMEM((MLA_ROWS, kc), F32), pltpu.VMEM((MLA_ROWS, kc), F32),
                        pltpu.VMEM((MLA_ROWS, kc), BF16), pltpu.VMEM((MLA_ROWS, kc), BF16),
                        pltpu.VMEM((MLA_ROWS, LANES), F32), pltpu.VMEM((MLA_ROWS, LANES), F32),
                        pltpu.VMEM((MLA_ROWS, LANES), F32), pltpu.VMEM((MLA_ROWS, 2 * LANES), F32)],
        compiler_params=_cparams(("parallel", "parallel")),
        name="mla_latent" if lat else "mla_context",
    )(*args)


def _merge_kernel(x_ref, *rest, alpha, n_lat_tiles, has_ctx):
    n_y = 2 * N_BRANCH if has_ctx else N_BRANCH
    y_refs, rest = rest[:n_y], rest[n_y:]
    (sh1_ref, sc1_ref, g1_ref, sh2_ref, sc2_ref, wg_ref, bg_ref, wb_ref, wo_ref, l1g_ref, l1b_ref,
     rw_ref, rb_ref, x1_ref, h2_ref, ei_ref, ew_ref) = rest
    tm = x_ref.shape[0]
    is_ctx = pl.program_id(0) >= n_lat_tiles
    x = x_ref[...]
    h = (_ln(x) * (1.0 + sc1_ref[...]) + sh1_ref[...]).astype(BF16)
    merged = None
    for i in range(N_BRANCH):
        yi = y_refs[i][...]
        if has_ctx:
            yi = jnp.where(is_ctx, y_refs[N_BRANCH + i][...], yi)
        gate = _sigmoid(jnp.dot(h, wg_ref[i], preferred_element_type=F32) + bg_ref[i])
        term = gate * jnp.dot(yi, wb_ref[i], preferred_element_type=F32)
        merged = term if merged is None else merged + term
    y = jnp.dot(merged.astype(BF16), wo_ref[...], preferred_element_type=F32)
    x1 = _ln(alpha * x + g1_ref[...] * y) * l1g_ref[...] + l1b_ref[...]
    x1_ref[...] = x1
    h2 = _ln(x1) * (1.0 + sc2_ref[...]) + sh2_ref[...]
    for j in range(ROW_TILES):
        h2_ref[pl.ds(j, tm, stride=ROW_TILES), :] = h2[:, LANES * j:LANES * (j + 1)]
    h2_hi = h2.astype(BF16)
    h2_lo = (h2 - h2_hi.astype(F32)).astype(BF16)
    t = jnp.dot(h2_hi, rw_ref[...], preferred_element_type=F32)
    lg = (t[:, 0:LANES] + t[:, LANES:2 * LANES]
          + jnp.dot(h2_lo, rw_ref[:, 0:LANES], preferred_element_type=F32) + rb_ref[...])
    lane = lax.broadcasted_iota(jnp.int32, (tm, LANES), 1).astype(F32)
    vals, idxs = [], []
    for _ in range(TOP_K):
        mx = jnp.max(lg, axis=-1, keepdims=True)
        ix = jnp.min(jnp.where(lg == mx, lane, float(LANES)), axis=-1, keepdims=True)
        vals.append(mx)
        idxs.append(ix)
        lg = jnp.where(lane == ix, -3.0e38, lg)
    exps = [jnp.exp(v - vals[0]) for v in vals]
    den = exps[0] + exps[1] + exps[2] + exps[3]
    ei = jnp.zeros((tm, LANES), F32)
    ew = jnp.zeros((tm, LANES), F32)
    for k in range(TOP_K):
        ei = jnp.where(lane == float(k), idxs[k], ei)
        ew = jnp.where(lane == float(k), exps[k] / den, ew)
    ei_ref[...] = ei.astype(jnp.int32)
    ew_ref[...] = ew


def _merge(x, ys_lat, ys_ctx, mods, lw, geo, alpha):
    tm, n_lat_tiles, tps, nb = geo["tm"], geo["n_lat_tiles"], geo["tps"], geo["B"]
    d = x.shape[1]
    has_ctx = ys_ctx is not None
    n_tiles = geo["n_tiles"] if has_ctx else n_lat_tiles
    n_ctx_tiles = geo["n_tiles"] - n_lat_tiles
    t_rows = n_tiles * tm

    def cond(i):
        return jnp.where(i < n_lat_tiles, i // tps, nb)

    row = lambda w: pl.BlockSpec((tm, w), lambda i: (i, 0))
    y_lat = pl.BlockSpec((tm, BRANCH_W), lambda i: (jnp.minimum(i, n_lat_tiles - 1), 0))
    y_ctx = pl.BlockSpec((tm, BRANCH_W), lambda i: (jnp.clip(i - n_lat_tiles, 0, n_ctx_tiles - 1), 0))
    y_specs = [y_lat] * N_BRANCH + ([y_ctx] * N_BRANCH if has_ctx else [])
    ys = tuple(ys_lat) + (tuple(ys_ctx) if has_ctx else ())
    mod = pl.BlockSpec((None, 1, d), lambda i: (cond(i), 0, 0))
    full = lambda shape: pl.BlockSpec(shape, lambda i: tuple(0 for _ in shape), pipeline_mode=pl.Buffered(1))
    return pl.pallas_call(
        functools.partial(_merge_kernel, alpha=alpha, n_lat_tiles=n_lat_tiles, has_ctx=has_ctx),
        grid=(n_tiles,),
        name="merge_router",
        in_specs=[row(d)] + y_specs + [mod, mod, mod, mod, mod,
                  full((4, d, d)), full((4, 1, d)), full((4, BRANCH_W, d)), full((d, d)),
                  full((1, d)), full((1, d)), full((d, 2 * LANES)), full((1, LANES))],
        out_specs=[row(d), pl.BlockSpec((tm * ROW_TILES, LANES), lambda i: (i, 0)), row(LANES), row(LANES)],
        out_shape=[jax.ShapeDtypeStruct((t_rows, d), F32),
                   jax.ShapeDtypeStruct((t_rows * ROW_TILES, LANES), F32),
                   jax.ShapeDtypeStruct((t_rows, LANES), jnp.int32),
                   jax.ShapeDtypeStruct((t_rows, LANES), F32)],
        compiler_params=_cparams(("parallel",)),
    )(x, *ys, mods["sh1"], mods["sc1"], mods["g1"], mods["sh2"], mods["sc2"],
      lw["w_gate"], lw["b_gate"], lw["w_branch"], lw["w_out"], lw["ln1_g"], lw["ln1_b"],
      lw["router_w"], lw["router_b"])


GMM_IDX_SLOTS = 8
GMM_X_SLOTS = 3
GMM_CAST_ROWS = 128


def _gmm_kernel(blk_e_ref, nvalid_ref, idx_hbm, h2_hbm, wgu_ref, bgu_ref, wdn_ref, bdn_ref,
                ytk_hbm, idx_sm, xbuf, ybuf, wgu_bf, wdn_bf, sem_i, sem_g, sem_s):
    b = pl.program_id(0)
    n_blocks = pl.num_programs(0)
    nvalid = nvalid_ref[0]
    bm = xbuf.shape[1] // ROW_TILES
    rows = bm * ROW_TILES
    pad0 = ytk_hbm.shape[0] - rows

    def idx_copy(blk):
        slot = blk % GMM_IDX_SLOTS
        return pltpu.make_async_copy(idx_hbm.at[jnp.minimum(blk, n_blocks - 1)], idx_sm.at[slot], sem_i.at[slot])

    def gather_row(blk, r):
        src = pl.multiple_of(idx_sm[blk % GMM_IDX_SLOTS, 0, r] * ROW_TILES, ROW_TILES)
        slot = blk % GMM_X_SLOTS
        pltpu.make_async_copy(h2_hbm.at[pl.ds(src, ROW_TILES), :],
                              xbuf.at[slot, pl.ds(r * ROW_TILES, ROW_TILES), :], sem_g.at[slot]).start()

    def scatter_row(blk, r):
        dst = pl.multiple_of(idx_sm[(blk + GMM_IDX_SLOTS) % GMM_IDX_SLOTS, 1, r] * ROW_TILES, ROW_TILES)
        slot = (blk + 2) % 2
        pltpu.make_async_copy(ybuf.at[slot, pl.ds(r * ROW_TILES, ROW_TILES), :],
                              ytk_hbm.at[pl.ds(dst, ROW_TILES), :], sem_s.at[slot]).start()

    def gather_wait(blk):
        slot = blk % GMM_X_SLOTS
        pltpu.make_async_copy(h2_hbm.at[pl.ds(0, rows), :], xbuf.at[slot], sem_g.at[slot]).wait()

    def scatter_wait(blk):
        slot = (blk + 2) % 2
        pltpu.make_async_copy(ybuf.at[slot], ytk_hbm.at[pl.ds(0, rows), :], sem_s.at[slot]).wait()

    def rolled(fn):
        def body(r, carry):
            fn(r)
            return carry
        lax.fori_loop(0, bm, body, 0)

    @pl.when(b == 0)
    def _():
        ybuf[1] = jnp.zeros(ybuf.shape[1:], F32)

        def fill(r):
            idx_sm[GMM_IDX_SLOTS - 1, 1, r] = pad0 // ROW_TILES + r

        rolled(fill)
        for blk in (0, 1):
            idx_copy(blk).start()
            idx_copy(blk).wait()
        idx_copy(2).start()
        rolled(lambda r: gather_row(0, r))
        rolled(lambda r: gather_row(1, r))

    @pl.when((b < nvalid) & ((b == 0) | (blk_e_ref[b] != blk_e_ref[jnp.maximum(b - 1, 0)])))
    def _():
        def cast(i, carry):
            rws = pl.ds(pl.multiple_of(i * GMM_CAST_ROWS, GMM_CAST_ROWS), GMM_CAST_ROWS)
            wgu_bf[rws, :] = wgu_ref[rws, :].astype(BF16)
            wdn_bf[rws, :] = wdn_ref[rws, :].astype(BF16)
            return carry
        lax.fori_loop(0, wgu_bf.shape[0] // GMM_CAST_ROWS, cast, 0)

    @pl.when(b < nvalid)
    def _():
        par = b % 2
        idx_copy(b + 3).start()
        idx_copy(b + 2).wait()
        gather_wait(b)
        xs = jnp.concatenate(
            [xbuf[b % GMM_X_SLOTS, pl.ds(j, bm, stride=ROW_TILES), :] for j in range(ROW_TILES)],
            axis=1).astype(BF16)
        for r in range(bm):
            gather_row(b + 2, r)
        for r in range(bm):
            scatter_row(b - 1, r)
        z = jnp.dot(xs, wgu_bf[...], preferred_element_type=F32) + bgu_ref[...]
        gate = jnp.minimum(z[:, :D_EXPERT], SWIGLU_LIMIT)
        up = jnp.clip(z[:, D_EXPERT:], -SWIGLU_LIMIT, SWIGLU_LIMIT)
        act = (up + 1.0) * gate * _sigmoid(SWIGLU_ALPHA * gate)
        y = jnp.dot(act.astype(BF16), wdn_bf[...], preferred_element_type=F32) + bdn_ref[...]
        for j in range(ROW_TILES):
            ybuf[par, pl.ds(j, bm, stride=ROW_TILES), :] = y[:, LANES * j:LANES * (j + 1)]
        scatter_wait(b - 1)

    @pl.when(b == nvalid - 1)
    def _():
        rolled(lambda r: scatter_row(b, r))
        scatter_wait(b)
        gather_wait(b + 1)
        gather_wait(b + 2)
        idx_copy(b + 3).wait()


def _gmm(h2g, plan, lw, n_tok, layer):
    bm = EXPERT_BM
    n_blocks = plan["idx"].shape[0]
    d = lw["w_gu"].shape[2]
    n_slots = n_tok * TOP_K + bm
    grid_spec = pltpu.PrefetchScalarGridSpec(
        num_scalar_prefetch=2,
        grid=(n_blocks,),
        in_specs=[
            pl.BlockSpec(memory_space=pl.ANY),
            pl.BlockSpec(memory_space=pl.ANY),
            pl.BlockSpec((None, None, d, 2 * D_EXPERT), lambda b, be, nv: (layer, be[b], 0, 0)),
            pl.BlockSpec((None, None, 1, 2 * D_EXPERT), lambda b, be, nv: (layer, be[b], 0, 0)),
            pl.BlockSpec((None, None, D_EXPERT, d), lambda b, be, nv: (layer, be[b], 0, 0)),
            pl.BlockSpec((None, None, 1, d), lambda b, be, nv: (layer, be[b], 0, 0)),
        ],
        out_specs=pl.BlockSpec(memory_space=pl.ANY),
        scratch_shapes=[pltpu.SMEM((GMM_IDX_SLOTS, 2, bm), jnp.int32),
                        pltpu.VMEM((GMM_X_SLOTS, bm * ROW_TILES, LANES), F32),
                        pltpu.VMEM((2, bm * ROW_TILES, LANES), F32),
                        pltpu.VMEM((d, 2 * D_EXPERT), BF16),
                        pltpu.VMEM((D_EXPERT, d), BF16),
                        pltpu.SemaphoreType.DMA((GMM_IDX_SLOTS,)),
                        pltpu.SemaphoreType.DMA((GMM_X_SLOTS,)),
                        pltpu.SemaphoreType.DMA((2,))],
    )
    return pl.pallas_call(
        _gmm_kernel,
        grid_spec=grid_spec,
        out_shape=jax.ShapeDtypeStruct((n_slots * ROW_TILES, LANES), F32),
        compiler_params=_cparams(("arbitrary",)),
        name="expert_gmm",
    )(plan["blk_e"], plan["nvalid"], plan["idx"], h2g, lw["w_gu"], lw["b_gu"], lw["w_down"], lw["b_down"])


def _plan(ei, n_tok):
    bm = EXPERT_BM
    n_assign = n_tok * TOP_K
    flat_e = ei.reshape(-1)
    id_bits = max(1, (n_assign - 1).bit_length())
    assert N_EXPERTS << id_bits < 2 ** 31
    packed = jnp.sort((flat_e << id_bits) | jnp.arange(n_assign, dtype=jnp.int32))
    order = packed & ((1 << id_bits) - 1)
    experts = jnp.arange(N_EXPERTS, dtype=jnp.int32)
    counts = jnp.sum((flat_e[:, None] == experts[None, :]).astype(jnp.int32), axis=0)
    padded = (counts + bm - 1) // bm * bm
    pad_end = jnp.cumsum(padded)
    pad_start = pad_end - padded
    grp_start = jnp.cumsum(counts) - counts
    n_blocks = -(-(n_assign + N_EXPERTS * (bm - 1)) // bm)
    blk_first = jnp.arange(n_blocks, dtype=jnp.int32) * bm
    blk_e = jnp.minimum(jnp.sum((pad_end[None, :] <= blk_first[:, None]).astype(jnp.int32), axis=1),
                        N_EXPERTS - 1)
    within = blk_first[:, None] + jnp.arange(bm, dtype=jnp.int32)[None, :] - pad_start[blk_e][:, None]
    valid = (within < counts[blk_e][:, None]) & (blk_first < pad_end[-1])[:, None]
    pos = jnp.clip(grp_start[blk_e][:, None] + within, 0, n_assign - 1)
    row_asg = order[pos]
    row_src = jnp.where(valid, row_asg // TOP_K, 0)
    row_dst = jnp.where(valid, (row_asg % TOP_K) * n_tok + row_asg // TOP_K,
                        n_assign + jnp.arange(bm, dtype=jnp.int32)[None, :])
    idx = jnp.stack([row_src, row_dst], axis=1).astype(jnp.int32)
    nvalid = (pad_end[-1:] // bm).astype(jnp.int32)
    return {"idx": idx, "blk_e": blk_e.astype(jnp.int32), "nvalid": nvalid}


def _combine_kernel(x_ref, y0_ref, y1_ref, y2_ref, y3_ref, ew_ref, g2_ref, lg_ref, lb_ref, o_ref, *, alpha):
    tm = x_ref.shape[0]
    ew = ew_ref[...]
    cols = []
    for j in range(ROW_TILES):
        acc = None
        for k, y_ref in enumerate((y0_ref, y1_ref, y2_ref, y3_ref)):
            piece = y_ref[pl.ds(j, tm, stride=ROW_TILES), :] * ew[:, k:k + 1]
            acc = piece if acc is None else acc + piece
        cols.append(acc)
    m = jnp.concatenate(cols, axis=1)
    o_ref[...] = _ln(alpha * x_ref[...] + g2_ref[...] * m) * lg_ref[...] + lb_ref[...]


def _combine(x1, ytk, ew, g2, lw, geo, n_tiles, alpha):
    tm, n_lat_tiles, tps, nb = geo["tmc"], geo["n_lat_tiles"], geo["tps"], geo["B"]
    ratio = geo["tm"] // tm
    d = x1.shape[1]

    def cond(i):
        return jnp.where(i < n_lat_tiles * ratio, i // (tps * ratio), nb)

    def slot_map(k, i):
        return (k * n_tiles * ratio + i, 0)

    row = lambda w: pl.BlockSpec((tm, w), lambda i: (i, 0))
    full = lambda shape: pl.BlockSpec(shape, lambda i: tuple(0 for _ in shape))
    return pl.pallas_call(
        functools.partial(_combine_kernel, alpha=alpha),
        grid=(n_tiles * ratio,),
        in_specs=[row(d)] + [pl.BlockSpec((tm * ROW_TILES, LANES), functools.partial(slot_map, k))
                             for k in range(TOP_K)] + [row(LANES),
                  pl.BlockSpec((None, 1, d), lambda i: (cond(i), 0, 0)), full((1, d)), full((1, d))],
        out_specs=row(d),
        out_shape=jax.ShapeDtypeStruct((n_tiles * tm * ratio, d), F32),
        compiler_params=_cparams(("parallel",)),
        name="moe_combine",
    )(x1, ytk, ytk, ytk, ytk, ew, g2, lw["ln2_g"], lw["ln2_b"])


def _rope_tables(seq_len, tm):
    rows = seq_len // GRID_W
    row = jnp.repeat(jnp.arange(rows, dtype=F32), GRID_W)
    col = jnp.tile(jnp.arange(GRID_W, dtype=F32), rows)

    def table(rot_dim, reps, pad_rows):
        n_freq = rot_dim // 4
        inv = ROPE_THETA ** (-jnp.arange(n_freq, dtype=F32) / n_freq)
        ang = jnp.concatenate([row[:, None] * inv, col[:, None] * inv], axis=-1)
        cos, sin = jnp.cos(ang), jnp.sin(ang)
        c = jnp.tile(jnp.concatenate([cos, cos], axis=-1), (1, reps))
        s = jnp.tile(jnp.concatenate([-sin, sin], axis=-1), (1, reps))
        c = jnp.concatenate([c, jnp.ones((pad_rows, c.shape[1]), F32)], axis=0)
        s = jnp.concatenate([s, jnp.zeros((pad_rows, s.shape[1]), F32)], axis=0)
        return c, s

    cs, sn = table(HEAD_DIM, 2, tm)
    ck, sk = table(MLA_ROPE, 4, tm)
    cq, sq = table(MLA_ROPE, 8, MLA_BLOCK)
    return {"cs": cs, "sn": sn, "ck": ck, "sk": sk, "cq": cq, "sq": sq}


def _layer_weights(l, w_in, conv_w, conv_b, conv_ln_g, conv_ln_b, swa_sink, pool_w, pool_scale, mla_q_g,
                   mla_w_uq, mla_kv_g, mla_w_uk, mla_w_uv, w_branch, w_gate, b_gate, w_out, ln1_g, ln1_b,
                   router_w, router_b, w_gu, b_gu, w_down, b_down, ln2_g, ln2_b):
    d = w_in.shape[1]
    wi = w_in[l]
    kr = wi[:, 2688:2720]
    w_all = jnp.concatenate([wi[:, :2688], kr, kr, kr, kr], axis=1).astype(BF16)
    wuq = mla_w_uq[l].reshape(MLA_Q_RANK, MLA_HEADS, MLA_NOPE + MLA_ROPE)
    wuq = jnp.concatenate([wuq[:, :, :MLA_NOPE].reshape(MLA_Q_RANK, -1),
                           wuq[:, :, MLA_NOPE:].reshape(MLA_Q_RANK, -1)], axis=1).astype(BF16)
    wuk = jnp.transpose(mla_w_uk[l], (1, 2, 0))
    wuv = jnp.transpose(mla_w_uv[l], (1, 0, 2))
    zk = jnp.zeros((MLA_NOPE, MLA_KV_RANK), F32)
    zv = jnp.zeros((MLA_KV_RANK, MLA_V), F32)
    wukp = jnp.stack([jnp.block([[wuk[2 * j], zk], [zk, wuk[2 * j + 1]]]) for j in range(MLA_HEADS // 2)])
    wuvp = jnp.stack([jnp.block([[wuv[2 * j], zv], [zv, wuv[2 * j + 1]]]) for j in range(MLA_HEADS // 2)])
    win_i = jnp.arange(CONV_ROWS + 2 * PAD_ROWS)
    rw = jnp.zeros((d, LANES), F32).at[:, :N_EXPERTS].set(router_w[l])
    rw_hi = rw.astype(BF16)
    rb = jnp.full((1, LANES), MASK_VALUE, F32).at[0, :N_EXPERTS].set(router_b[l])
    return {
        "w_all": w_all,
        "conv_w": jnp.repeat(jnp.concatenate([conv_w[l], jnp.zeros((1, CONV_CH), F32)], axis=0), SUBLANES, axis=0),
        "conv_shift": (win_i[None, None, :] == win_i[None, :, None] + jnp.arange(SUBLANES)[:, None, None]).astype(BF16),
        "conv_b": conv_b[l][None], "conv_ln_g": conv_ln_g[l][None], "conv_ln_b": conv_ln_b[l][None],
        "sink": jnp.broadcast_to(swa_sink[l][:, None] * LOG2E, (SWA_HEADS, LANES)),
        "pool_w": pool_w[l].astype(BF16), "pool_scale": pool_scale[l][None],
        "mla_q_g": mla_q_g[l][None], "mla_kv_g": mla_kv_g[l][None],
        "wuq": wuq, "wukp": wukp.astype(BF16), "wuvp": wuvp.astype(BF16),
        "w_branch": w_branch[l].astype(BF16), "w_gate": w_gate[l].astype(BF16),
        "b_gate": b_gate[l][:, None, :], "w_out": w_out[l].astype(BF16),
        "ln1_g": ln1_g[l][None], "ln1_b": ln1_b[l][None],
        "router_w": jnp.concatenate([rw_hi, (rw - rw_hi.astype(F32)).astype(BF16)], axis=1), "router_b": rb,
        "w_gu": w_gu, "b_gu": b_gu[:, :, None, :],
        "w_down": w_down, "b_down": b_down[:, :, None, :],
        "ln2_g": ln2_g[l][None], "ln2_b": ln2_b[l][None],
    }


def _token_tile(seq_len, ctx_rows):
    for tm in (512, 256, 128):
        if seq_len % tm == 0 and ctx_rows % tm == 0:
            return tm
    raise ValueError("unsupported sequence / context lengths")


def kernel(x, c, ctx, c_ctx, w_ada, b_ada, w_in, conv_w, conv_b, conv_ln_g, conv_ln_b, swa_sink, pool_w,
           pool_scale, mla_q_g, mla_w_uq, mla_kv_g, mla_w_uk, mla_w_uv, w_branch, w_gate, b_gate, w_out,
           ln1_g, ln1_b, router_w, router_b, w_gu, b_gu, w_down, b_down, ln2_g, ln2_b):
    nb, ls, d = x.shape
    lc = ctx.shape[1]
    depth = w_ada.shape[0]
    assert ls % GRID_W == 0 and ls % SWA_BLOCK == 0 and (nb * ls) % lc == 0
    assert ls % min(MLA_KC, ls) == 0 and lc % min(MLA_KX, lc) == 0 and lc % MLA_BLOCK == 0
    tm = _token_tile(ls, nb * lc)
    t_lat, t_ctx = nb * ls, nb * lc
    geo = {"B": nb, "L": ls, "C": lc, "tm": tm, "tmc": min(tm, 256), "tps": ls // tm,
           "n_lat_tiles": t_lat // tm, "n_tiles": (t_lat + t_ctx) // tm}
    alpha = (2 * depth) ** 0.25

    ra = -(-(nb + 1) // SUBLANES) * SUBLANES
    cvec = jnp.zeros((ra, d), F32).at[:nb].set(c).at[nb].set(c_ctx)
    ada = _ada_terms(cvec, w_ada, b_ada).reshape(depth, ra, 6, 1, d)
    tabs = _rope_tables(ls, tm)

    xa = jnp.concatenate([x.reshape(t_lat, d), ctx.reshape(t_ctx, d)], axis=0)
    for l in range(depth):
        ctx_out = l < depth - 1
        n_tiles = geo["n_tiles"] if ctx_out else geo["n_lat_tiles"]
        names = ("sh1", "sc1", "g1", "sh2", "sc2", "g2")
        mods = {n: ada[l, :, i] for i, n in enumerate(names)}
        lw = _layer_weights(l, w_in, conv_w, conv_b, conv_ln_g, conv_ln_b, swa_sink, pool_w, pool_scale,
                            mla_q_g, mla_w_uq, mla_kv_g, mla_w_uk, mla_w_uv, w_branch, w_gate, b_gate,
                            w_out, ln1_g, ln1_b, router_w, router_b, w_gu, b_gu, w_down, b_down,
                            ln2_g, ln2_b)
        a, q, k2, v2, p, cq, kc = _in_proj(xa, mods["sh1"], mods["sc1"], lw["w_all"], tabs,
                                           lw["mla_kv_g"], geo)
        ya, yc = _local_mixers(a, p, lw, ls, 0, nb)
        yb, yb_ctx = _swa(q, k2, v2, lw["sink"], geo, ctx_out)
        yd = _mla(cq, kc, lw, tabs, geo, True)
        ys_ctx = None
        if ctx_out:
            ya_ctx, yc_ctx = _local_mixers(a, p, lw, lc, t_lat // lc, nb)
            yd_ctx = _mla(cq, kc, lw, tabs, geo, False)
            ys_ctx = (ya_ctx, yb_ctx, yc_ctx, yd_ctx)
        x1, h2g, ei, ew = _merge(xa, (ya, yb, yc, yd), ys_ctx, mods, lw, geo, alpha)
        n_tok = n_tiles * tm
        plan = _plan(ei[:, :TOP_K], n_tok)
        ytk = _gmm(h2g, plan, lw, n_tok, l)
        xa = _combine(x1, ytk, ew, mods["g2"], lw, geo, n_tiles, alpha)
    return xa[:t_lat].reshape(nb, ls, d)
```

```python
import functools

import jax
import jax.numpy as jnp
import numpy as np
from jax import lax
from jax.experimental import pallas as pl
from jax.experimental.pallas import tpu as pltpu

F32 = jnp.float32
BF16 = jnp.bfloat16
HIGHEST = lax.Precision.HIGHEST

GRID_W = 64
HEAD_DIM = 64
ROPE_THETA = 10000.0
LN_EPS = 1e-6
MASK_VALUE = -1e30
CONV_CH = 512
CONV_K = 31
SWA_HEADS = 8
SWA_WINDOW = 128
SWA_BLOCK = 128
POOL_CH = 512
POOL_WINDOWS = (2, 4, 8, 16)
POOL_GROUP = 128
MLA_HEADS = 8
MLA_Q_RANK = 256
MLA_KV_RANK = 128
MLA_NOPE = 64
MLA_ROPE = 32
MLA_V = 64
MLA_BLOCK = 128
N_BRANCH = 4
BRANCH_W = 512
N_EXPERTS = 32
TOP_K = 4
D_EXPERT = 1024
SWIGLU_LIMIT = 7.0
SWIGLU_ALPHA = 1.702

LOG2E = 1.4426950408889634

LANES = 128
SUBLANES = 8
ROW_TILES = 8
EXPERT_BM = 512
VMEM_LIMIT = 56 * 1024 * 1024


def _cparams(sem):
    return pltpu.CompilerParams(dimension_semantics=sem, vmem_limit_bytes=VMEM_LIMIT)


def _ln(x):
    mu = jnp.mean(x, axis=-1, keepdims=True)
    xc = x - mu
    var = jnp.mean(xc * xc, axis=-1, keepdims=True)
    return xc * lax.rsqrt(var + LN_EPS)


def _sigmoid(x):
    return 0.5 * jnp.tanh(0.5 * x) + 0.5


def _ada_kernel(c_ref, w_ref, b_ref, o_ref):
    c = c_ref[...]
    s = c * _sigmoid(c)
    o_ref[...] = jnp.dot(s, w_ref[...], precision=HIGHEST, preferred_element_type=F32) + b_ref[...]


def _ada_terms(cvec, w_ada, b_ada):
    depth, d, n = w_ada.shape
    ra = cvec.shape[0]
    tn = 1536
    return pl.pallas_call(
        _ada_kernel,
        grid=(depth, n // tn),
        in_specs=[
            pl.BlockSpec((ra, d), lambda l, j: (0, 0)),
            pl.BlockSpec((None, d, tn), lambda l, j: (l, 0, j)),
            pl.BlockSpec((None, 1, tn), lambda l, j: (l, 0, j)),
        ],
        out_specs=pl.BlockSpec((None, ra, tn), lambda l, j: (l, 0, j)),
        out_shape=jax.ShapeDtypeStruct((depth, ra, n), F32),
        compiler_params=_cparams(("parallel", "parallel")),
        name="ada_terms",
    )(cvec, w_ada, b_ada.reshape(depth, 1, n))


IN_COLS = 1024 + 512 + 128 + 128 + 512 + 256 + 128 + 128


def _in_kernel(x_ref, sh_ref, sc_ref, w_ref, cs_ref, sn_ref, ck_ref, sk_ref, kvg_ref,
               a_ref, q_ref, k2_ref, v2_ref, p_ref, cq_ref, kc_ref):
    tm = x_ref.shape[0]
    h = _ln(x_ref[...]) * (1.0 + sc_ref[...]) + sh_ref[...]
    z = jnp.dot(h.astype(BF16), w_ref[...], preferred_element_type=F32)
    a_ref[...] = z[:, 0:1024].astype(BF16)
    cs, sn = cs_ref[...], sn_ref[...]
    lane = lax.broadcasted_iota(jnp.int32, (tm, LANES), 1)
    first32 = (lane % 64) < 32

    def rope64(t):
        rot = jnp.where(first32, pltpu.roll(t, LANES - 32, 1), pltpu.roll(t, 32, 1))
        return t * cs + rot * sn

    for j in range(4):
        qj = rope64(z[:, 1024 + LANES * j:1024 + LANES * (j + 1)]) * (HEAD_DIM ** -0.5 * LOG2E)
        q_ref[:, LANES * j:LANES * (j + 1)] = qj.astype(BF16)
    low64 = lane < 64
    k = rope64(z[:, 1536:1664])
    ks = pltpu.roll(k, 64, 1)
    k2_ref[:, 0:LANES] = jnp.where(low64, k, ks).astype(BF16)
    k2_ref[:, LANES:2 * LANES] = jnp.where(low64, ks, k).astype(BF16)
    v = z[:, 1664:1792]
    vs = pltpu.roll(v, 64, 1)
    v2_ref[:, 0:LANES] = jnp.where(low64, v, vs).astype(BF16)
    v2_ref[:, LANES:2 * LANES] = jnp.where(low64, vs, v).astype(BF16)
    p_ref[...] = z[:, 1792:2304].astype(BF16)
    cq_ref[...] = z[:, 2304:2560].astype(BF16)
    ckv = z[:, 2560:2688]
    ckvn = ckv * lax.rsqrt(jnp.mean(ckv * ckv, axis=-1, keepdims=True) + LN_EPS) * kvg_ref[...]
    kr4 = z[:, 2688:2816]
    first16 = (lane % 32) < 16
    rot = jnp.where(first16, pltpu.roll(kr4, LANES - 16, 1), pltpu.roll(kr4, 16, 1))
    kr4 = kr4 * ck_ref[...] + rot * sk_ref[...]
    kc_ref[:, 0:LANES] = ckvn.astype(BF16)
    kc_ref[:, LANES:2 * LANES] = kr4.astype(BF16)


def _in_proj(x, sh, sc, w_all, tabs, kvg, geo):
    tm, n_tiles, n_lat_tiles, tps, nb = geo["tm"], geo["n_tiles"], geo["n_lat_tiles"], geo["tps"], geo["B"]
    t_rows, d = n_tiles * tm, x.shape[1]

    def cond(i):
        return jnp.where(i < n_lat_tiles, i // tps, nb)

    def pos(i):
        return jnp.where(i < n_lat_tiles, i % tps, tps)

    row = lambda w: pl.BlockSpec((tm, w), lambda i: (i, 0))
    mod = pl.BlockSpec((None, 1, d), lambda i: (cond(i), 0, 0))
    tab = pl.BlockSpec((tm, LANES), lambda i: (pos(i), 0))
    widths = (1024, 512, 256, 256, 512, 256, 256)
    return pl.pallas_call(
        _in_kernel,
        grid=(n_tiles,),
        in_specs=[row(d), mod, mod,
                  pl.BlockSpec((d, IN_COLS), lambda i: (0, 0)),
                  tab, tab, tab, tab,
                  pl.BlockSpec((1, LANES), lambda i: (0, 0))],
        out_specs=[row(w) for w in widths],
        out_shape=[jax.ShapeDtypeStruct((t_rows, w), BF16) for w in widths],
        compiler_params=_cparams(("parallel",)),
        name="in_proj",
    )(x, sh, sc, w_all, tabs["cs"], tabs["sn"], tabs["ck"], tabs["sk"], kvg)


CONV_ROWS = 32
CONV_UNROLL = 4
PAD_ROWS = 16


def _local_kernel(a_ref, p_ref, cw_ref, sh_ref, cb_ref, lg_ref, lb_ref, pw_ref, ps_ref, ya_ref, yc_ref, vpad, upad,
                  *, ls):
    ch = min(256, ls)
    zeros = jnp.zeros((PAD_ROWS, CONV_CH), F32)
    vpad[0:PAD_ROWS, :] = zeros
    vpad[ls + PAD_ROWS:ls + 2 * PAD_ROWS, :] = zeros
    upad[0:PAD_ROWS, :] = zeros
    upad[ls + PAD_ROWS:ls + 2 * PAD_ROWS, :] = zeros

    def fill(c, carry):
        r0 = pl.multiple_of(c * ch, ch)
        a = a_ref[pl.ds(r0, ch), :].astype(F32)
        vpad[pl.ds(r0 + PAD_ROWS, ch), :] = a[:, :CONV_CH] * _sigmoid(a[:, CONV_CH:])
        upad[pl.ds(r0 + PAD_ROWS, ch), :] = p_ref[pl.ds(r0, ch), :].astype(F32)
        return carry

    lax.fori_loop(0, ls // ch, fill, 0)

    cb, lg, lb = cb_ref[...], lg_ref[...], lb_ref[...]
    groups = CONV_ROWS // SUBLANES

    def conv_chunk(r0):
        win = vpad[pl.ds(r0, CONV_ROWS + 2 * PAD_ROWS), :].astype(BF16)
        accs = [jnp.zeros((SUBLANES, CONV_CH), F32) for _ in range(groups)]
        for r in range(SUBLANES):
            wr = win.astype(F32) if r == 0 else jnp.dot(sh_ref[r], win, preferred_element_type=F32)
            for a in range(4):
                k = SUBLANES * a + r - 1
                if 0 <= k < CONV_K:
                    w8 = cw_ref[SUBLANES * k:SUBLANES * (k + 1), :]
                    for g in range(groups):
                        lo = SUBLANES * (a + g)
                        accs[g] = accs[g] + wr[lo:lo + SUBLANES, :] * w8
        y = _ln(jnp.concatenate(accs, axis=0) + cb) * lg + lb
        ya_ref[pl.ds(r0, CONV_ROWS), :] = (y * _sigmoid(y)).astype(BF16)

    def conv(c, carry):
        for u in range(CONV_UNROLL):
            conv_chunk(pl.multiple_of((c * CONV_UNROLL + u) * CONV_ROWS, CONV_ROWS))
        return carry

    lax.fori_loop(0, ls // (CONV_ROWS * CONV_UNROLL), conv, 0)

    ps = ps_ref[...]

    def pool(c, carry):
        r0 = pl.multiple_of(c * ch, ch)
        t = r0 + lax.broadcasted_iota(jnp.int32, (ch, LANES), 0)
        for g, w in enumerate(POOL_WINDOWS):
            lanes = slice(LANES * g, LANES * (g + 1))
            uwin = upad[pl.ds(r0, ch + 2 * PAD_ROWS), lanes]
            s = None
            for o in range(PAD_ROWS - w // 2, PAD_ROWS + w // 2):
                piece = uwin[o:o + ch, :]
                s = piece if s is None else s + piece
            lo = jnp.maximum(t - w // 2, 0)
            hi = jnp.minimum(t + w // 2, ls)
            d = s / (hi - lo).astype(F32) - uwin[PAD_ROWS:PAD_ROWS + ch, :]
            y = jnp.dot(d.astype(BF16), pw_ref[g], preferred_element_type=F32) * ps[:, lanes]
            yc_ref[pl.ds(r0, ch), lanes] = y.astype(BF16)
        return carry

    lax.fori_loop(0, ls // ch, pool, 0)


def _local_mixers(a, p, lw, ls, blk0, nseq):
    seq_in = lambda w: pl.BlockSpec((ls, w), lambda b: (blk0 + b, 0), pipeline_mode=pl.Buffered(1))
    seq_out = pl.BlockSpec((ls, 512), lambda b: (b, 0))
    full = lambda shape: pl.BlockSpec(shape, lambda b: tuple(0 for _ in shape))
    return pl.pallas_call(
        functools.partial(_local_kernel, ls=ls),
        grid=(nseq,),
        in_specs=[seq_in(1024), seq_in(512), full((SUBLANES * 32, CONV_CH)),
                  full((SUBLANES, CONV_ROWS + 2 * PAD_ROWS, CONV_ROWS + 2 * PAD_ROWS)), full((1, CONV_CH)), full((1, CONV_CH)),
                  full((1, CONV_CH)), full((4, POOL_GROUP, POOL_GROUP)), full((1, POOL_CH))],
        out_specs=[seq_out, seq_out],
        out_shape=[jax.ShapeDtypeStruct((nseq * ls, 512), BF16)] * 2,
        scratch_shapes=[pltpu.VMEM((ls + 2 * PAD_ROWS, CONV_CH), F32),
                        pltpu.VMEM((ls + 2 * PAD_ROWS, POOL_CH), F32)],
        compiler_params=_cparams(("parallel",)),
        name="local_mixers",
    )(a, p, lw["conv_w"], lw["conv_shift"], lw["conv_b"], lw["conv_ln_g"], lw["conv_ln_b"], lw["pool_w"],
      lw["pool_scale"])


def _swa_heads(q, kk, vv, bias, sink_ref, o_ref):
    bq = q.shape[0]
    lane = lax.broadcasted_iota(jnp.int32, (bq, LANES), 1)
    low = lane < 64
    zero = jnp.zeros((), BF16)
    one = jnp.ones((), BF16)
    dn = (((1,), (1,)), ((), ()))
    lane_k = lax.broadcasted_iota(jnp.int32, (kk.shape[0], LANES), 1) < 64
    for g in range(2):
        pairs = (q[:, LANES * (2 * g):LANES * (2 * g + 1)], q[:, LANES * (2 * g + 1):LANES * (2 * g + 2)])
        qs = jnp.concatenate([jnp.where(low, pairs[0], zero), jnp.where(low, pairs[1], zero),
                              jnp.where(low, zero, pairs[0]), jnp.where(low, zero, pairs[1])], axis=0)
        s = lax.dot_general(qs, kk[:, LANES * g:LANES * (g + 1)], dn, preferred_element_type=F32)
        if bias is not None:
            s = s + jnp.concatenate([bias] * 4, axis=0)
        heads = (4 * g, 4 * g + 2, 4 * g + 1, 4 * g + 3)
        sk = jnp.concatenate([jnp.broadcast_to(sink_ref[h:h + 1, :], (bq, LANES)) for h in heads], axis=0)
        m = jnp.maximum(jnp.max(s, axis=-1, keepdims=True), sk)
        e_sink = jnp.exp2(sk - m)
        p = jnp.concatenate([jnp.exp2(s[:, LANES * t:LANES * (t + 1)] - m).astype(BF16)
                             for t in range(s.shape[1] // LANES)], axis=1)
        vg = vv[:, LANES * g:LANES * (g + 1)]
        o_even = jnp.dot(p[:2 * bq], jnp.where(lane_k, vg, one), preferred_element_type=F32)
        o_odd = jnp.dot(p[2 * bq:], jnp.where(lane_k, one, vg), preferred_element_type=F32)
        out_even = o_even * pltpu.roll(1.0 / (o_even + e_sink[:2 * bq]), 64, 1)
        out_odd = o_odd * pltpu.roll(1.0 / (o_odd + e_sink[2 * bq:]), 64, 1)
        for jj in range(2):
            rows = slice(jj * bq, (jj + 1) * bq)
            o_ref[:, LANES * (2 * g + jj):LANES * (2 * g + jj + 1)] = jnp.where(
                low, out_even[rows], out_odd[rows]).astype(BF16)


def _swa_lat_kernel(q_ref, kp_ref, kc_ref, kn_ref, vp_ref, vc_ref, vn_ref, kx_ref, vx_ref, sink_ref,
                    o_ref, *, seq_len):
    i = pl.program_id(1)
    kk = jnp.concatenate([kp_ref[...], kc_ref[...], kn_ref[...], kx_ref[...]], axis=0)
    vv = jnp.concatenate([vp_ref[...], vc_ref[...], vn_ref[...], vx_ref[...]], axis=0)
    nk = kk.shape[0]
    r = lax.broadcasted_iota(jnp.int32, (SWA_BLOCK, nk), 0)
    s = lax.broadcasted_iota(jnp.int32, (SWA_BLOCK, nk), 1)
    kpos = (i - 1) * SWA_BLOCK + s
    diff = kpos - (i * SWA_BLOCK + r)
    ok = ((kpos >= 0) & (kpos < seq_len) & (jnp.abs(diff) <= SWA_WINDOW)) | (s >= 3 * SWA_BLOCK)
    _swa_heads(q_ref[...], kk, vv, jnp.where(ok, 0.0, MASK_VALUE), sink_ref, o_ref)


def _swa_ctx_kernel(q_ref, kx_ref, vx_ref, sink_ref, o_ref):
    _swa_heads(q_ref[...], kx_ref[...], vx_ref[...], None, sink_ref, o_ref)


def _swa(q, k2, v2, sink_b, geo, with_ctx):
    nb, ls, lc = geo["B"], geo["L"], geo["C"]
    nq = ls // SWA_BLOCK
    ctx_blk0 = nb * ls // lc
    qspec = pl.BlockSpec((SWA_BLOCK, 512), lambda b, i: (b * nq + i, 0))
    kv = lambda off: pl.BlockSpec(
        (SWA_BLOCK, 256), lambda b, i: (b * nq + jnp.clip(i + off, 0, nq - 1), 0))
    cx = pl.BlockSpec((lc, 256), lambda b, i: (ctx_blk0 + b, 0))
    sk = pl.BlockSpec((SUBLANES, LANES), lambda b, i: (0, 0))
    yb = pl.pallas_call(
        functools.partial(_swa_lat_kernel, seq_len=ls),
        grid=(nb, nq),
        in_specs=[qspec, kv(-1), kv(0), kv(1), kv(-1), kv(0), kv(1), cx, cx, sk],
        out_specs=qspec,
        out_shape=jax.ShapeDtypeStruct((nb * ls, 512), BF16),
        compiler_params=_cparams(("parallel", "parallel")),
        name="swa_latent",
    )(q, k2, k2, k2, v2, v2, v2, k2, v2, sink_b)
    if not with_ctx:
        return yb, None
    cx1 = pl.BlockSpec((lc, 256), lambda b: (ctx_blk0 + b, 0))
    yb_ctx = pl.pallas_call(
        _swa_ctx_kernel,
        grid=(nb,),
        in_specs=[pl.BlockSpec((lc, 512), lambda b: (ctx_blk0 + b, 0)), cx1, cx1,
                  pl.BlockSpec((SUBLANES, LANES), lambda b: (0, 0))],
        out_specs=pl.BlockSpec((lc, 512), lambda b: (b, 0)),
        out_shape=jax.ShapeDtypeStruct((nb * lc, 512), BF16),
        compiler_params=_cparams(("parallel",)),
        name="swa_context",
    )(q, k2, v2, sink_b)
    return yb, yb_ctx


MLA_KC = 512
MLA_ROWS = MLA_HEADS * MLA_BLOCK
MLA_KX = 256
MLA_SOFTMAX_ROWS = 16


def _mla_kernel(cq_ref, *rest, n_lat, n_ctx, kc, kx):
    if n_lat:
        kl_ref, rest = rest[0], rest[1:]
    (kx_ref, qg_ref, wuq_ref, wukp_ref, wuvp_ref, cs_ref, sn_ref, o_ref,
     q_sc, s_a, s_b, p_a, p_b, al_a, al_b, m_sc, acc_sc) = rest
    n = n_lat + n_ctx
    scale = (MLA_NOPE + MLA_ROPE) ** -0.5 * LOG2E
    cq = cq_ref[...].astype(F32)
    cqn = cq * lax.rsqrt(jnp.mean(cq * cq, axis=-1, keepdims=True) + LN_EPS) * qg_ref[...]
    q = jnp.dot(cqn.astype(BF16), wuq_ref[...], preferred_element_type=F32)
    qr = q[:, 512:768]
    lane2 = lax.broadcasted_iota(jnp.int32, (MLA_BLOCK, 2 * LANES), 1)
    rot = jnp.where((lane2 % 32) < 16, pltpu.roll(qr, 2 * LANES - 16, 1), pltpu.roll(qr, 16, 1))
    qr = (qr * cs_ref[...] + rot * sn_ref[...]) * scale
    slot_of_lane = lax.broadcasted_iota(jnp.int32, (MLA_BLOCK, LANES), 1) // 32
    for j in range(MLA_HEADS // 2):
        qa2 = jnp.dot(q[:, LANES * j:LANES * (j + 1)].astype(BF16), wukp_ref[j],
                      preferred_element_type=F32) * scale
        for e in range(2):
            h = 2 * j + e
            rows = slice(h * MLA_BLOCK, (h + 1) * MLA_BLOCK)
            qrh = jnp.where(slot_of_lane == (h % 4), qr[:, LANES * (h // 4):LANES * (h // 4 + 1)], 0.0)
            q_sc[rows, 0:LANES] = qa2[:, LANES * e:LANES * (e + 1)].astype(BF16)
            q_sc[rows, LANES:2 * LANES] = qrh.astype(BF16)

    m_sc[...] = jnp.full(m_sc.shape, MASK_VALUE, F32)
    acc_sc[...] = jnp.zeros(acc_sc.shape, F32)
    dn = (((1,), (1,)), ((), ()))
    s_bufs, p_bufs, al_bufs = (s_a, s_b), (p_a, p_b), (al_a, al_b)

    def width(c):
        return kx if isinstance(c, int) and c >= n_lat else kc

    def keys(c):
        if isinstance(c, int):
            if c >= n_lat:
                return kx_ref[(c - n_lat) * kx:(c - n_lat + 1) * kx, :]
            return kl_ref[c * kc:(c + 1) * kc, :]
        return kl_ref[pl.ds(pl.multiple_of(c * kc, kc), kc), :]

    def scores(c, par):
        s_bufs[par][:, 0:width(c)] = lax.dot_general(q_sc[...], keys(c), dn, preferred_element_type=F32)

    def softmax(c, par):
        s_ref, p_ref, al_ref = s_bufs[par], p_bufs[par], al_bufs[par]
        w = width(c)
        for rb in range(MLA_ROWS // MLA_SOFTMAX_ROWS):
            rows = slice(rb * MLA_SOFTMAX_ROWS, (rb + 1) * MLA_SOFTMAX_ROWS)
            sv = s_ref[rows, 0:w]
            m_old = m_sc[rows, :]
            m_new = jnp.maximum(m_old, jnp.max(sv, axis=-1, keepdims=True))
            al_ref[rows, :] = jnp.exp2(m_old - m_new)
            m_sc[rows, :] = m_new
            for t in range(w // LANES):
                lanes = slice(LANES * t, LANES * (t + 1))
                p_ref[rows, lanes] = jnp.exp2(sv[:, lanes] - m_new).astype(BF16)

    def values(c, par):
        w = width(c)
        v_aug = jnp.concatenate([keys(c)[:, 0:LANES], jnp.ones((w, LANES), BF16)], axis=1)
        pv = jnp.dot(p_bufs[par][:, 0:w], v_aug, preferred_element_type=F32)
        al = al_bufs[par][...]
        acc_sc[...] = jnp.concatenate([al, al], axis=1) * acc_sc[...] + pv

    def step(t, par):
        static = isinstance(t, int)
        if not static or t < n:
            scores(t, par)
        if not static or 1 <= t <= n:
            softmax(t - 1, 1 - par)
        if not static or 2 <= t <= n + 1:
            values(t - 2, par)

    loop_lo = 2
    loop_hi = max(loop_lo, n_lat)
    n_pairs = (loop_hi - loop_lo) // 2
    for t in range(0, loop_lo):
        step(t, t % 2)
    if n_pairs:
        def pair(jp, carry):
            t0 = loop_lo + 2 * jp
            step(t0, 0)
            step(t0 + 1, 1)
            return carry
        lax.fori_loop(0, n_pairs, pair, 0)
    for t in range(loop_lo + 2 * n_pairs, n + 2):
        step(t, t % 2)

    acc = acc_sc[...]
    o = acc[:, 0:LANES] / acc[:, LANES:2 * LANES]
    for j in range(MLA_HEADS // 2):
        o2 = jnp.concatenate([o[(2 * j) * MLA_BLOCK:(2 * j + 1) * MLA_BLOCK, :],
                              o[(2 * j + 1) * MLA_BLOCK:(2 * j + 2) * MLA_BLOCK, :]], axis=1)
        o_ref[:, LANES * j:LANES * (j + 1)] = jnp.dot(
            o2.astype(BF16), wuvp_ref[j], preferred_element_type=F32).astype(BF16)


def _mla(cq, kc_all, lw, tabs, geo, lat):
    nb, ls, lc = geo["B"], geo["L"], geo["C"]
    ctx_blk0 = nb * ls // lc
    if lat:
        nq = ls // MLA_BLOCK
        qmap = lambda b, i: (b * nq + i, 0)
        tmap = lambda b, i: (i, 0)
        kc = min(MLA_KC, ls)
        n_lat = ls // kc
    else:
        nq = lc // MLA_BLOCK
        qmap = lambda b, i: (nb * (ls // MLA_BLOCK) + b * nq + i, 0)
        tmap = lambda b, i: (ls // MLA_BLOCK, 0)
        kc, n_lat = MLA_KX, 0
    kx = min(MLA_KX, lc)
    omap = lambda b, i: (b * nq + i, 0)
    full = lambda shape: pl.BlockSpec(shape, lambda b, i: tuple(0 for _ in shape))
    in_specs = [pl.BlockSpec((MLA_BLOCK, 256), qmap)]
    args = [cq]
    if lat:
        in_specs.append(pl.BlockSpec((ls, 256), lambda b, i: (b, 0)))
        args.append(kc_all)
    in_specs += [pl.BlockSpec((lc, 256), lambda b, i: (ctx_blk0 + b, 0)),
                 full((1, 256)), full((256, 768)), full((4, LANES, 256)), full((4, 256, LANES)),
                 pl.BlockSpec((MLA_BLOCK, 256), tmap), pl.BlockSpec((MLA_BLOCK, 256), tmap)]
    args += [kc_all, lw["mla_q_g"], lw["wuq"], lw["wukp"], lw["wuvp"], tabs["cq"], tabs["sq"]]
    return pl.pallas_call(
        functools.partial(_mla_kernel, n_lat=n_lat, n_ctx=lc // kx, kc=kc, kx=kx),
        grid=(nb, nq),
        in_specs=in_specs,
        out_specs=pl.BlockSpec((MLA_BLOCK, 512), omap),
        out_shape=jax.ShapeDtypeStruct((nb * nq * MLA_BLOCK, 512), BF16),
        scratch_shapes=[pltpu.VMEM((MLA_ROWS, 256), BF16),
                        pltpu.VMEM((MLA_ROWS, kc), F32), pltpu.VMEM((MLA_ROWS, kc), F32),
                        pltpu.VMEM((MLA_ROWS, kc), BF16), pltpu.VMEM((MLA_ROWS, kc), BF16),
                        pltpu.VMEM((MLA_ROWS, LANES), F32), pltpu.VMEM((MLA_ROWS, LANES), F32),
                        pltpu.VMEM((MLA_ROWS, LANES), F32), pltpu.VMEM((MLA_ROWS, 2 * LANES), F32)],
        compiler_params=_cparams(("parallel", "parallel")),
        name="mla_latent" if lat else "mla_context",
    )(*args)


def _merge_kernel(x_ref, *rest, alpha, n_lat_tiles, has_ctx):
    n_y = 2 * N_BRANCH if has_ctx else N_BRANCH
    y_refs, rest = rest[:n_y], rest[n_y:]
    (sh1_ref, sc1_ref, g1_ref, sh2_ref, sc2_ref, wg_ref, bg_ref, wb_ref, wo_ref, l1g_ref, l1b_ref,
     rw_ref, rb_ref, x1_ref, h2_ref, ei_ref, ew_ref) = rest
    tm = x_ref.shape[0]
    is_ctx = pl.program_id(0) >= n_lat_tiles
    x = x_ref[...]
    h = (_ln(x) * (1.0 + sc1_ref[...]) + sh1_ref[...]).astype(BF16)
    merged = None
    for i in range(N_BRANCH):
        yi = y_refs[i][...]
        if has_ctx:
            yi = jnp.where(is_ctx, y_refs[N_BRANCH + i][...], yi)
        gate = _sigmoid(jnp.dot(h, wg_ref[i], preferred_element_type=F32) + bg_ref[i])
        term = gate * jnp.dot(yi, wb_ref[i], preferred_element_type=F32)
        merged = term if merged is None else merged + term
    y = jnp.dot(merged.astype(BF16), wo_ref[...], preferred_element_type=F32)
    x1 = _ln(alpha * x + g1_ref[...] * y) * l1g_ref[...] + l1b_ref[...]
    x1_ref[...] = x1
    h2 = _ln(x1) * (1.0 + sc2_ref[...]) + sh2_ref[...]
    for j in range(ROW_TILES):
        h2_ref[pl.ds(j, tm, stride=ROW_TILES), :] = h2[:, LANES * j:LANES * (j + 1)]
    h2_hi = h2.astype(BF16)
    h2_lo = (h2 - h2_hi.astype(F32)).astype(BF16)
    t = jnp.dot(h2_hi, rw_ref[...], preferred_element_type=F32)
    lg = (t[:, 0:LANES] + t[:, LANES:2 * LANES]
          + jnp.dot(h2_lo, rw_ref[:, 0:LANES], preferred_element_type=F32) + rb_ref[...])
    lane = lax.broadcasted_iota(jnp.int32, (tm, LANES), 1).astype(F32)
    vals, idxs = [], []
    for _ in range(TOP_K):
        mx = jnp.max(lg, axis=-1, keepdims=True)
        ix = jnp.min(jnp.where(lg == mx, lane, float(LANES)), axis=-1, keepdims=True)
        vals.append(mx)
        idxs.append(ix)
        lg = jnp.where(lane == ix, -3.0e38, lg)
    exps = [jnp.exp(v - vals[0]) for v in vals]
    den = exps[0] + exps[1] + exps[2] + exps[3]
    ei = jnp.zeros((tm, LANES), F32)
    ew = jnp.zeros((tm, LANES), F32)
    for k in range(TOP_K):
        ei = jnp.where(lane == float(k), idxs[k], ei)
        ew = jnp.where(lane == float(k), exps[k] / den, ew)
    ei_ref[...] = ei.astype(jnp.int32)
    ew_ref[...] = ew


def _merge(x, ys_lat, ys_ctx, mods, lw, geo, alpha):
    tm, n_lat_tiles, tps, nb = geo["tm"], geo["n_lat_tiles"], geo["tps"], geo["B"]
    d = x.shape[1]
    has_ctx = ys_ctx is not None
    n_tiles = geo["n_tiles"] if has_ctx else n_lat_tiles
    n_ctx_tiles = geo["n_tiles"] - n_lat_tiles
    t_rows = n_tiles * tm

    def cond(i):
        return jnp.where(i < n_lat_tiles, i // tps, nb)

    row = lambda w: pl.BlockSpec((tm, w), lambda i: (i, 0))
    y_lat = pl.BlockSpec((tm, BRANCH_W), lambda i: (jnp.minimum(i, n_lat_tiles - 1), 0))
    y_ctx = pl.BlockSpec((tm, BRANCH_W), lambda i: (jnp.clip(i - n_lat_tiles, 0, n_ctx_tiles - 1), 0))
    y_specs = [y_lat] * N_BRANCH + ([y_ctx] * N_BRANCH if has_ctx else [])
    ys = tuple(ys_lat) + (tuple(ys_ctx) if has_ctx else ())
    mod = pl.BlockSpec((None, 1, d), lambda i: (cond(i), 0, 0))
    full = lambda shape: pl.BlockSpec(shape, lambda i: tuple(0 for _ in shape), pipeline_mode=pl.Buffered(1))
    return pl.pallas_call(
        functools.partial(_merge_kernel, alpha=alpha, n_lat_tiles=n_lat_tiles, has_ctx=has_ctx),
        grid=(n_tiles,),
        name="merge_router",
        in_specs=[row(d)] + y_specs + [mod, mod, mod, mod, mod,
                  full((4, d, d)), full((4, 1, d)), full((4, BRANCH_W, d)), full((d, d)),
                  full((1, d)), full((1, d)), full((d, 2 * LANES)), full((1, LANES))],
        out_specs=[row(d), pl.BlockSpec((tm * ROW_TILES, LANES), lambda i: (i, 0)), row(LANES), row(LANES)],
        out_shape=[jax.ShapeDtypeStruct((t_rows, d), F32),
                   jax.ShapeDtypeStruct((t_rows * ROW_TILES, LANES), F32),
                   jax.ShapeDtypeStruct((t_rows, LANES), jnp.int32),
                   jax.ShapeDtypeStruct((t_rows, LANES), F32)],
        compiler_params=_cparams(("parallel",)),
    )(x, *ys, mods["sh1"], mods["sc1"], mods["g1"], mods["sh2"], mods["sc2"],
      lw["w_gate"], lw["b_gate"], lw["w_branch"], lw["w_out"], lw["ln1_g"], lw["ln1_b"],
      lw["router_w"], lw["router_b"])


GMM_IDX_SLOTS = 8
GMM_X_SLOTS = 3
GMM_CAST_ROWS = 128


def _gmm_kernel(blk_e_ref, nvalid_ref, idx_hbm, h2_hbm, wgu_ref, bgu_ref, wdn_ref, bdn_ref,
                ytk_hbm, idx_sm, xbuf, ybuf, wgu_bf, wdn_bf, sem_i, sem_g, sem_s):
    b = pl.program_id(0)
    n_blocks = pl.num_programs(0)
    nvalid = nvalid_ref[0]
    bm = xbuf.shape[1] // ROW_TILES
    rows = bm * ROW_TILES
    pad0 = ytk_hbm.shape[0] - rows

    def idx_copy(blk):
        slot = blk % GMM_IDX_SLOTS
        return pltpu.make_async_copy(idx_hbm.at[jnp.minimum(blk, n_blocks - 1)], idx_sm.at[slot], sem_i.at[slot])

    def gather_row(blk, r):
        src = pl.multiple_of(idx_sm[blk % GMM_IDX_SLOTS, 0, r] * ROW_TILES, ROW_TILES)
        slot = blk % GMM_X_SLOTS
        pltpu.make_async_copy(h2_hbm.at[pl.ds(src, ROW_TILES), :],
                              xbuf.at[slot, pl.ds(r * ROW_TILES, ROW_TILES), :], sem_g.at[slot]).start(
                                  priority=r % 2 if isinstance(r, int) else 0)

    def scatter_row(blk, r):
        dst = pl.multiple_of(idx_sm[(blk + GMM_IDX_SLOTS) % GMM_IDX_SLOTS, 1, r] * ROW_TILES, ROW_TILES)
        slot = (blk + 2) % 2
        pltpu.make_async_copy(ybuf.at[slot, pl.ds(r * ROW_TILES, ROW_TILES), :],
                              ytk_hbm.at[pl.ds(dst, ROW_TILES), :], sem_s.at[slot]).start(
                                  priority=r % 2 if isinstance(r, int) else 0)

    def gather_wait(blk):
        slot = blk % GMM_X_SLOTS
        pltpu.make_async_copy(h2_hbm.at[pl.ds(0, rows), :], xbuf.at[slot], sem_g.at[slot]).wait()

    def scatter_wait(blk):
        slot = (blk + 2) % 2
        pltpu.make_async_copy(ybuf.at[slot], ytk_hbm.at[pl.ds(0, rows), :], sem_s.at[slot]).wait()

    def rolled(fn):
        def body(r, carry):
            fn(r)
            return carry
        lax.fori_loop(0, bm, body, 0)

    @pl.when(b == 0)
    def _():
        ybuf[1] = jnp.zeros(ybuf.shape[1:], F32)

        def fill(r):
            idx_sm[GMM_IDX_SLOTS - 1, 1, r] = pad0 // ROW_TILES + r

        rolled(fill)
        for blk in (0, 1):
            idx_copy(blk).start()
            idx_copy(blk).wait()
        idx_copy(2).start()
        rolled(lambda r: gather_row(0, r))
        rolled(lambda r: gather_row(1, r))

    @pl.when((b < nvalid) & ((b == 0) | (blk_e_ref[b] != blk_e_ref[jnp.maximum(b - 1, 0)])))
    def _():
        def cast(i, carry):
            rws = pl.ds(pl.multiple_of(i * GMM_CAST_ROWS, GMM_CAST_ROWS), GMM_CAST_ROWS)
            wgu_bf[rws, :] = wgu_ref[rws, :].astype(BF16)
            wdn_bf[rws, :] = wdn_ref[rws, :].astype(BF16)
            return carry
        lax.fori_loop(0, wgu_bf.shape[0] // GMM_CAST_ROWS, cast, 0)

    @pl.when(b < nvalid)
    def _():
        par = b % 2
        idx_copy(b + 3).start()
        idx_copy(b + 2).wait()
        gather_wait(b)
        xs = jnp.concatenate(
            [xbuf[b % GMM_X_SLOTS, pl.ds(j, bm, stride=ROW_TILES), :] for j in range(ROW_TILES)],
            axis=1).astype(BF16)
        for r in range(bm):
            gather_row(b + 2, r)
        for r in range(bm):
            scatter_row(b - 1, r)
        z = jnp.dot(xs, wgu_bf[...], preferred_element_type=F32) + bgu_ref[...]
        gate = jnp.minimum(z[:, :D_EXPERT], SWIGLU_LIMIT)
        up = jnp.clip(z[:, D_EXPERT:], -SWIGLU_LIMIT, SWIGLU_LIMIT)
        act = (up + 1.0) * gate * _sigmoid(SWIGLU_ALPHA * gate)
        y = jnp.dot(act.astype(BF16), wdn_bf[...], preferred_element_type=F32) + bdn_ref[...]
        for j in range(ROW_TILES):
            ybuf[par, pl.ds(j, bm, stride=ROW_TILES), :] = y[:, LANES * j:LANES * (j + 1)]
        scatter_wait(b - 1)

    @pl.when(b == nvalid - 1)
    def _():
        rolled(lambda r: scatter_row(b, r))
        scatter_wait(b)
        gather_wait(b + 1)
        gather_wait(b + 2)
        idx_copy(b + 3).wait()


def _gmm(h2g, plan, lw, n_tok, layer):
    bm = EXPERT_BM
    n_blocks = plan["idx"].shape[0]
    d = lw["w_gu"].shape[2]
    n_slots = n_tok * TOP_K + bm
    grid_spec = pltpu.PrefetchScalarGridSpec(
        num_scalar_prefetch=2,
        grid=(n_blocks,),
        in_specs=[
            pl.BlockSpec(memory_space=pl.ANY),
            pl.BlockSpec(memory_space=pl.ANY),
            pl.BlockSpec((None, None, d, 2 * D_EXPERT), lambda b, be, nv: (layer, be[b], 0, 0)),
            pl.BlockSpec((None, None, 1, 2 * D_EXPERT), lambda b, be, nv: (layer, be[b], 0, 0)),
            pl.BlockSpec((None, None, D_EXPERT, d), lambda b, be, nv: (layer, be[b], 0, 0)),
            pl.BlockSpec((None, None, 1, d), lambda b, be, nv: (layer, be[b], 0, 0)),
        ],
        out_specs=pl.BlockSpec(memory_space=pl.ANY),
        scratch_shapes=[pltpu.SMEM((GMM_IDX_SLOTS, 2, bm), jnp.int32),
                        pltpu.VMEM((GMM_X_SLOTS, bm * ROW_TILES, LANES), F32),
                        pltpu.VMEM((2, bm * ROW_TILES, LANES), F32),
                        pltpu.VMEM((d, 2 * D_EXPERT), BF16),
                        pltpu.VMEM((D_EXPERT, d), BF16),
                        pltpu.SemaphoreType.DMA((GMM_IDX_SLOTS,)),
                        pltpu.SemaphoreType.DMA((GMM_X_SLOTS,)),
                        pltpu.SemaphoreType.DMA((2,))],
    )
    return pl.pallas_call(
        _gmm_kernel,
        grid_spec=grid_spec,
        out_shape=jax.ShapeDtypeStruct((n_slots * ROW_TILES, LANES), F32),
        compiler_params=_cparams(("arbitrary",)),
        name="expert_gmm",
    )(plan["blk_e"], plan["nvalid"], plan["idx"], h2g, lw["w_gu"], lw["b_gu"], lw["w_down"], lw["b_down"])


def _plan(ei, n_tok):
    bm = EXPERT_BM
    n_assign = n_tok * TOP_K
    flat_e = ei.reshape(-1)
    id_bits = max(1, (n_assign - 1).bit_length())
    assert N_EXPERTS << id_bits < 2 ** 31
    packed = jnp.sort((flat_e << id_bits) | jnp.arange(n_assign, dtype=jnp.int32))
    order = packed & ((1 << id_bits) - 1)
    experts = jnp.arange(N_EXPERTS, dtype=jnp.int32)
    counts = jnp.sum((flat_e[:, None] == experts[None, :]).astype(jnp.int32), axis=0)
    padded = (counts + bm - 1) // bm * bm
    pad_end = jnp.cumsum(padded)
    pad_start = pad_end - padded
    grp_start = jnp.cumsum(counts) - counts
    n_blocks = -(-(n_assign + N_EXPERTS * (bm - 1)) // bm)
    blk_first = jnp.arange(n_blocks, dtype=jnp.int32) * bm
    blk_e = jnp.minimum(jnp.sum((pad_end[None, :] <= blk_first[:, None]).astype(jnp.int32), axis=1),
                        N_EXPERTS - 1)
    within = blk_first[:, None] + jnp.arange(bm, dtype=jnp.int32)[None, :] - pad_start[blk_e][:, None]
    valid = (within < counts[blk_e][:, None]) & (blk_first < pad_end[-1])[:, None]
    pos = jnp.clip(grp_start[blk_e][:, None] + within, 0, n_assign - 1)
    row_asg = order[pos]
    row_src = jnp.where(valid, row_asg // TOP_K, 0)
    row_dst = jnp.where(valid, (row_asg % TOP_K) * n_tok + row_asg // TOP_K,
                        n_assign + jnp.arange(bm, dtype=jnp.int32)[None, :])
    idx = jnp.stack([row_src, row_dst], axis=1).astype(jnp.int32)
    nvalid = (pad_end[-1:] // bm).astype(jnp.int32)
    return {"idx": idx, "blk_e": blk_e.astype(jnp.int32), "nvalid": nvalid}


def _combine_kernel(x_ref, y0_ref, y1_ref, y2_ref, y3_ref, ew_ref, g2_ref, lg_ref, lb_ref, o_ref, *, alpha):
    tm = x_ref.shape[0]
    ew = ew_ref[...]
    cols = []
    for j in range(ROW_TILES):
        acc = None
        for k, y_ref in enumerate((y0_ref, y1_ref, y2_ref, y3_ref)):
            piece = y_ref[pl.ds(j, tm, stride=ROW_TILES), :] * ew[:, k:k + 1]
            acc = piece if acc is None else acc + piece
        cols.append(acc)
    m = jnp.concatenate(cols, axis=1)
    o_ref[...] = _ln(alpha * x_ref[...] + g2_ref[...] * m) * lg_ref[...] + lb_ref[...]


def _combine(x1, ytk, ew, g2, lw, geo, n_tiles, alpha):
    tm, n_lat_tiles, tps, nb = geo["tmc"], geo["n_lat_tiles"], geo["tps"], geo["B"]
    ratio = geo["tm"] // tm
    d = x1.shape[1]

    def cond(i):
        return jnp.where(i < n_lat_tiles * ratio, i // (tps * ratio), nb)

    def slot_map(k, i):
        return (k * n_tiles * ratio + i, 0)

    row = lambda w: pl.BlockSpec((tm, w), lambda i: (i, 0))
    full = lambda shape: pl.BlockSpec(shape, lambda i: tuple(0 for _ in shape))
    return pl.pallas_call(
        functools.partial(_combine_kernel, alpha=alpha),
        grid=(n_tiles * ratio,),
        in_specs=[row(d)] + [pl.BlockSpec((tm * ROW_TILES, LANES), functools.partial(slot_map, k))
                             for k in range(TOP_K)] + [row(LANES),
                  pl.BlockSpec((None, 1, d), lambda i: (cond(i), 0, 0)), full((1, d)), full((1, d))],
        out_specs=row(d),
        out_shape=jax.ShapeDtypeStruct((n_tiles * tm * ratio, d), F32),
        compiler_params=_cparams(("parallel",)),
        name="moe_combine",
    )(x1, ytk, ytk, ytk, ytk, ew, g2, lw["ln2_g"], lw["ln2_b"])


def _rope_tables(seq_len, tm):
    rows = seq_len // GRID_W
    row = jnp.repeat(jnp.arange(rows, dtype=F32), GRID_W)
    col = jnp.tile(jnp.arange(GRID_W, dtype=F32), rows)

    def table(rot_dim, reps, pad_rows):
        n_freq = rot_dim // 4
        inv = ROPE_THETA ** (-jnp.arange(n_freq, dtype=F32) / n_freq)
        ang = jnp.concatenate([row[:, None] * inv, col[:, None] * inv], axis=-1)
        cos, sin = jnp.cos(ang), jnp.sin(ang)
        c = jnp.tile(jnp.concatenate([cos, cos], axis=-1), (1, reps))
        s = jnp.tile(jnp.concatenate([-sin, sin], axis=-1), (1, reps))
        c = jnp.concatenate([c, jnp.ones((pad_rows, c.shape[1]), F32)], axis=0)
        s = jnp.concatenate([s, jnp.zeros((pad_rows, s.shape[1]), F32)], axis=0)
        return c, s

    cs, sn = table(HEAD_DIM, 2, tm)
    ck, sk = table(MLA_ROPE, 4, tm)
    cq, sq = table(MLA_ROPE, 8, MLA_BLOCK)
    return {"cs": cs, "sn": sn, "ck": ck, "sk": sk, "cq": cq, "sq": sq}


def _layer_weights(l, w_in, conv_w, conv_b, conv_ln_g, conv_ln_b, swa_sink, pool_w, pool_scale, mla_q_g,
                   mla_w_uq, mla_kv_g, mla_w_uk, mla_w_uv, w_branch, w_gate, b_gate, w_out, ln1_g, ln1_b,
                   router_w, router_b, w_gu, b_gu, w_down, b_down, ln2_g, ln2_b):
    d = w_in.shape[1]
    wi = w_in[l]
    kr = wi[:, 2688:2720]
    w_all = jnp.concatenate([wi[:, :2688], kr, kr, kr, kr], axis=1).astype(BF16)
    wuq = mla_w_uq[l].reshape(MLA_Q_RANK, MLA_HEADS, MLA_NOPE + MLA_ROPE)
    wuq = jnp.concatenate([wuq[:, :, :MLA_NOPE].reshape(MLA_Q_RANK, -1),
                           wuq[:, :, MLA_NOPE:].reshape(MLA_Q_RANK, -1)], axis=1).astype(BF16)
    wuk = jnp.transpose(mla_w_uk[l], (1, 2, 0))
    wuv = jnp.transpose(mla_w_uv[l], (1, 0, 2))
    zk = jnp.zeros((MLA_NOPE, MLA_KV_RANK), F32)
    zv = jnp.zeros((MLA_KV_RANK, MLA_V), F32)
    wukp = jnp.stack([jnp.block([[wuk[2 * j], zk], [zk, wuk[2 * j + 1]]]) for j in range(MLA_HEADS // 2)])
    wuvp = jnp.stack([jnp.block([[wuv[2 * j], zv], [zv, wuv[2 * j + 1]]]) for j in range(MLA_HEADS // 2)])
    win_i = jnp.arange(CONV_ROWS + 2 * PAD_ROWS)
    rw = jnp.zeros((d, LANES), F32).at[:, :N_EXPERTS].set(router_w[l])
    rw_hi = rw.astype(BF16)
    rb = jnp.full((1, LANES), MASK_VALUE, F32).at[0, :N_EXPERTS].set(router_b[l])
    return {
        "w_all": w_all,
        "conv_w": jnp.repeat(jnp.concatenate([conv_w[l], jnp.zeros((1, CONV_CH), F32)], axis=0), SUBLANES, axis=0),
        "conv_shift": (win_i[None, None, :] == win_i[None, :, None] + jnp.arange(SUBLANES)[:, None, None]).astype(BF16),
        "conv_b": conv_b[l][None], "conv_ln_g": conv_ln_g[l][None], "conv_ln_b": conv_ln_b[l][None],
        "sink": jnp.broadcast_to(swa_sink[l][:, None] * LOG2E, (SWA_HEADS, LANES)),
        "pool_w": pool_w[l].astype(BF16), "pool_scale": pool_scale[l][None],
        "mla_q_g": mla_q_g[l][None], "mla_kv_g": mla_kv_g[l][None],
        "wuq": wuq, "wukp": wukp.astype(BF16), "wuvp": wuvp.astype(BF16),
        "w_branch": w_branch[l].astype(BF16), "w_gate": w_gate[l].astype(BF16),
        "b_gate": b_gate[l][:, None, :], "w_out": w_out[l].astype(BF16),
        "ln1_g": ln1_g[l][None], "ln1_b": ln1_b[l][None],
        "router_w": jnp.concatenate([rw_hi, (rw - rw_hi.astype(F32)).astype(BF16)], axis=1), "router_b": rb,
        "w_gu": w_gu, "b_gu": b_gu[:, :, None, :],
        "w_down": w_down, "b_down": b_down[:, :, None, :],
        "ln2_g": ln2_g[l][None], "ln2_b": ln2_b[l][None],
    }


def _token_tile(seq_len, ctx_rows):
    for tm in (512, 256, 128):
        if seq_len % tm == 0 and ctx_rows % tm == 0:
            return tm
    raise ValueError("unsupported sequence / context lengths")


def kernel(x, c, ctx, c_ctx, w_ada, b_ada, w_in, conv_w, conv_b, conv_ln_g, conv_ln_b, swa_sink, pool_w,
           pool_scale, mla_q_g, mla_w_uq, mla_kv_g, mla_w_uk, mla_w_uv, w_branch, w_gate, b_gate, w_out,
           ln1_g, ln1_b, router_w, router_b, w_gu, b_gu, w_down, b_down, ln2_g, ln2_b):
    nb, ls, d = x.shape
    lc = ctx.shape[1]
    depth = w_ada.shape[0]
    assert ls % GRID_W == 0 and ls % SWA_BLOCK == 0 and (nb * ls) % lc == 0
    assert ls % min(MLA_KC, ls) == 0 and lc % min(MLA_KX, lc) == 0 and lc % MLA_BLOCK == 0
    tm = _token_tile(ls, nb * lc)
    t_lat, t_ctx = nb * ls, nb * lc
    geo = {"B": nb, "L": ls, "C": lc, "tm": tm, "tmc": min(tm, 256), "tps": ls // tm,
           "n_lat_tiles": t_lat // tm, "n_tiles": (t_lat + t_ctx) // tm}
    alpha = (2 * depth) ** 0.25

    ra = -(-(nb + 1) // SUBLANES) * SUBLANES
    cvec = jnp.zeros((ra, d), F32).at[:nb].set(c).at[nb].set(c_ctx)
    ada = _ada_terms(cvec, w_ada, b_ada).reshape(depth, ra, 6, 1, d)
    tabs = _rope_tables(ls, tm)

    xa = jnp.concatenate([x.reshape(t_lat, d), ctx.reshape(t_ctx, d)], axis=0)
    for l in range(depth):
        ctx_out = l < depth - 1
        n_tiles = geo["n_tiles"] if ctx_out else geo["n_lat_tiles"]
        names = ("sh1", "sc1", "g1", "sh2", "sc2", "g2")
        mods = {n: ada[l, :, i] for i, n in enumerate(names)}
        lw = _layer_weights(l, w_in, conv_w, conv_b, conv_ln_g, conv_ln_b, swa_sink, pool_w, pool_scale,
                            mla_q_g, mla_w_uq, mla_kv_g, mla_w_uk, mla_w_uv, w_branch, w_gate, b_gate,
                            w_out, ln1_g, ln1_b, router_w, router_b, w_gu, b_gu, w_down, b_down,
                            ln2_g, ln2_b)
        a, q, k2, v2, p, cq, kc = _in_proj(xa, mods["sh1"], mods["sc1"], lw["w_all"], tabs,
                                           lw["mla_kv_g"], geo)
        ya, yc = _local_mixers(a, p, lw, ls, 0, nb)
        yb, yb_ctx = _swa(q, k2, v2, lw["sink"], geo, ctx_out)
        yd = _mla(cq, kc, lw, tabs, geo, True)
        ys_ctx = None
        if ctx_out:
            ya_ctx, yc_ctx = _local_mixers(a, p, lw, lc, t_lat // lc, nb)
            yd_ctx = _mla(cq, kc, lw, tabs, geo, False)
            ys_ctx = (ya_ctx, yb_ctx, yc_ctx, yd_ctx)
        x1, h2g, ei, ew = _merge(xa, (ya, yb, yc, yd), ys_ctx, mods, lw, geo, alpha)
        n_tok = n_tiles * tm
        plan = _plan(ei[:, :TOP_K], n_tok)
        ytk = _gmm(h2g, plan, lw, n_tok, l)
        xa = _combine(x1, ytk, ew, mods["g2"], lw, geo, n_tiles, alpha)
    return xa[:t_lat].reshape(nb, ls, d)
```
